```python
import math
import jax, jax.numpy as jnp
from jax import lax
import numpy as np

D_MODEL = 4096
BATCH = 4
SEQ = 2048
DEPTH = 1
DEC_BATCH = 32
DEC_SEQ = 8
PAST_LEN = 8192
PAGE_SIZE = 128

HEAD_DIM = 128
ATTN_WIDTH = D_MODEL // 2
N_Q_HEADS = ATTN_WIDTH // HEAD_DIM
N_KV_HEADS = N_Q_HEADS // 2
KV_GROUP = N_Q_HEADS // N_KV_HEADS
KV_WIDTH = N_KV_HEADS * HEAD_DIM
MOBA_BLOCK = 256
MOBA_TOPK = 3
Q_CHUNK = 64
HG_WIDTH = D_MODEL - ATTN_WIDTH
HG_KDIM = 128
HG_VDIM = 128
HG_HEADS = HG_WIDTH // HG_VDIM
HG_KWIDTH = HG_HEADS * HG_KDIM
HG_CHUNK = 64
MIX_WIDTH = ATTN_WIDTH + HG_WIDTH
D_FF = 4 * D_MODEL
IN_WIDTH = ATTN_WIDTH + 2 * KV_WIDTH + 2 * HG_KWIDTH + 2 * HG_WIDTH
SPLITS = [ATTN_WIDTH,
          ATTN_WIDTH + KV_WIDTH,
          ATTN_WIDTH + 2 * KV_WIDTH,
          ATTN_WIDTH + 2 * KV_WIDTH + HG_KWIDTH,
          ATTN_WIDTH + 2 * KV_WIDTH + 2 * HG_KWIDTH,
          ATTN_WIDTH + 2 * KV_WIDTH + 2 * HG_KWIDTH + HG_WIDTH]
EPS = 1e-6
NEG = -1e30

kernel_name = "hymba_moba_hgrn2_decode_step"


def rms_norm(x, g):
    xf = x.astype(jnp.float32)
    y = xf * lax.rsqrt(jnp.mean(xf * xf, axis=-1, keepdims=True) + EPS)
    return y.astype(x.dtype) * g


def mixer_inputs(x, norm1_g, w_in, q_norm_g, k_norm_g, lb):
    B, T, _ = x.shape
    z = rms_norm(x, norm1_g) @ w_in
    q, k, v, hq, hf, hi, hg = jnp.split(z, SPLITS, axis=-1)
    q = rms_norm(q.reshape(B, T, N_Q_HEADS, HEAD_DIM), q_norm_g)
    k = rms_norm(k.reshape(B, T, N_KV_HEADS, HEAD_DIM), k_norm_g)
    v = v.reshape(B, T, N_KV_HEADS, HEAD_DIM)
    lbh = lb.reshape(HG_HEADS, HG_KDIM)
    forget = lbh + (1.0 - lbh) * jax.nn.sigmoid(hf.astype(jnp.float32).reshape(B, T, HG_HEADS, HG_KDIM))
    hg_logf = jnp.log(forget)
    hg_k = 1.0 - forget
    hg_q = jax.nn.silu(hq).reshape(B, T, HG_HEADS, HG_KDIM)
    hg_v = hi.reshape(B, T, HG_HEADS, HG_VDIM)
    return q, k, v, hg_q, hg_k, hg_v, hg_logf, hg


def mixer_output(x, attn_o, hg_o, hg_gate, hg_norm_g, w_out, norm2_g, w_up, w_down):
    B, T, _ = x.shape
    hg_o = rms_norm(hg_o, hg_norm_g.reshape(HG_HEADS, HG_VDIM)).reshape(B, T, HG_WIDTH)
    hg_o = hg_o * jax.nn.sigmoid(hg_gate)
    mix = jnp.concatenate([attn_o, hg_o.astype(attn_o.dtype)], axis=-1)
    x = x + mix @ w_out
    h = rms_norm(x, norm2_g)
    return x + jnp.square(jax.nn.relu(h @ w_up)) @ w_down


def hgrn2_scan(q, k, v, logf, h0):
    B, T, H, dk = q.shape
    C = HG_CHUNK if T % HG_CHUNK == 0 else T
    n = T // C

    def chunks(a):
        return a.astype(jnp.float32).reshape(B, n, C, H, a.shape[-1]).transpose(1, 0, 3, 2, 4)

    tri = jnp.tril(jnp.ones((C, C), bool))[None, None, :, :, None]

    def step(h, inp):
        qi, ki, vi, gi = inp
        b = jnp.cumsum(gi, axis=2)
        diff = b[:, :, :, None, :] - b[:, :, None, :, :]
        decay = jnp.where(tri, jnp.exp(jnp.where(tri, diff, 0.0)), 0.0)
        att = jnp.einsum('bhtd,bhtsd,bhsd->bhts', qi, decay, ki)
        o = jnp.einsum('bhtd,bhde->bhte', qi * jnp.exp(b), h) + jnp.einsum('bhts,bhse->bhte', att, vi)
        b_last = b[:, :, -1:, :]
        h = jnp.exp(b_last[:, :, 0, :, None]) * h + jnp.einsum('bhsd,bhse->bhde', ki * jnp.exp(b_last - b), vi)
        return h, o

    h, o = lax.scan(step, h0.astype(jnp.float32), (chunks(q), chunks(k), chunks(v), chunks(logf)))
    o = o.transpose(1, 0, 3, 2, 4).reshape(B, T, H, v.shape[-1])
    return o.astype(v.dtype), h


def moba_select(q, kmean, n_past, k_sel):
    B, T = q.shape[:2]
    nB = kmean.shape[1]
    qg = q.astype(jnp.float32).reshape(B, T, N_KV_HEADS, KV_GROUP, HEAD_DIM)
    s = jnp.einsum('btngd,bmnd->btngm', qg, kmean).reshape(B, T, N_Q_HEADS, nB)
    valid = jnp.arange(nB)[None, :] < n_past[:, None]
    s = jnp.where(valid[None, :, None, :], s, NEG)
    _, idx = lax.top_k(s, k_sel)
    sel_valid = idx < n_past[None, :, None, None]
    return idx, sel_valid


def moba_attend(q, qpos, sel_k, sel_v, sel_valid, own_k, own_v, own_kpos):
    B, Q = q.shape[:2]
    qf = q.astype(jnp.float32) * (HEAD_DIM ** -0.5)
    qg = qf.reshape(B, Q, N_KV_HEADS, KV_GROUP, HEAD_DIM)
    ok, ov = own_k.astype(jnp.float32), own_v.astype(jnp.float32)
    s_own = jnp.einsum('bqngd,blnd->bqngl', qg, ok).reshape(B, Q, N_Q_HEADS, -1)
    causal = own_kpos[None, :] <= qpos[:, None]
    s_own = jnp.where(causal[None, :, None, :], s_own, NEG)
    Lo = s_own.shape[-1]
    if sel_k is None:
        p_own = jax.nn.softmax(s_own, axis=-1)
        out = jnp.einsum('bqngl,blnd->bqngd', p_own.reshape(B, Q, N_KV_HEADS, KV_GROUP, Lo), ov)
    else:
        s_sel = jnp.einsum('bqhd,bqhld->bqhl', qf, sel_k.astype(jnp.float32))
        s_sel = jnp.where(sel_valid, s_sel, NEG)
        Ls = s_sel.shape[-1]
        p = jax.nn.softmax(jnp.concatenate([s_sel, s_own], axis=-1), axis=-1)
        p_sel, p_own = p[..., :Ls], p[..., Ls:]
        out = jnp.einsum('bqhl,bqhld->bqhd', p_sel, sel_v.astype(jnp.float32))
        out = out + jnp.einsum('bqngl,blnd->bqngd', p_own.reshape(B, Q, N_KV_HEADS, KV_GROUP, Lo), ov).reshape(B, Q, N_Q_HEADS, HEAD_DIM)
    return out.reshape(B, Q, ATTN_WIDTH).astype(q.dtype)


def moba_prompt(q, k, v):
    B, S = q.shape[:2]
    nB = -(-S // MOBA_BLOCK)
    pad = nB * MOBA_BLOCK - S
    kp = jnp.pad(k, ((0, 0), (0, pad), (0, 0), (0, 0))).reshape(B, nB, MOBA_BLOCK, N_KV_HEADS, HEAD_DIM)
    vp = jnp.pad(v, ((0, 0), (0, pad), (0, 0), (0, 0))).reshape(B, nB, MOBA_BLOCK, N_KV_HEADS, HEAD_DIM)
    pos = jnp.arange(S)
    k_sel = min(MOBA_TOPK, nB - 1)
    if k_sel > 0:
        kmean = jnp.mean(kp.astype(jnp.float32), axis=2)
        idx, valid = moba_select(q, kmean, pos // MOBA_BLOCK, k_sel)
    bi = jnp.arange(B)[:, None, None, None]
    hk = (jnp.arange(N_Q_HEADS) // KV_GROUP)[None, None, :, None]

    def chunk(i):
        start = i * Q_CHUNK
        qc = lax.dynamic_slice_in_dim(q, start, Q_CHUNK, axis=1)
        qpos = start + jnp.arange(Q_CHUNK)
        blk = start // MOBA_BLOCK
        own_k = lax.dynamic_index_in_dim(kp, blk, axis=1, keepdims=False)
        own_v = lax.dynamic_index_in_dim(vp, blk, axis=1, keepdims=False)
        own_kpos = blk * MOBA_BLOCK + jnp.arange(MOBA_BLOCK)
        if k_sel == 0:
            return moba_attend(qc, qpos, None, None, None, own_k, own_v, own_kpos)
        ic = lax.dynamic_slice_in_dim(idx, start, Q_CHUNK, axis=1)
        vc = lax.dynamic_slice_in_dim(valid, start, Q_CHUNK, axis=1)
        sel_k = kp[bi, ic, :, hk].reshape(B, Q_CHUNK, N_Q_HEADS, k_sel * MOBA_BLOCK, HEAD_DIM)
        sel_v = vp[bi, ic, :, hk].reshape(B, Q_CHUNK, N_Q_HEADS, k_sel * MOBA_BLOCK, HEAD_DIM)
        sel_valid = jnp.repeat(vc, MOBA_BLOCK, axis=-1)
        return moba_attend(qc, qpos, sel_k, sel_v, sel_valid, own_k, own_v, own_kpos)

    out = lax.map(chunk, jnp.arange(S // Q_CHUNK))
    return out.transpose(1, 0, 2, 3).reshape(B, S, ATTN_WIDTH)


def moba_sample(q, k_new, v_new, ck, cv, page_table):
    B, T = q.shape[:2]
    ppb = MOBA_BLOCK // PAGE_SIZE
    n_full = PAST_LEN // MOBA_BLOCK
    own_start = n_full * MOBA_BLOCK
    n_own_pages = (PAST_LEN - own_start) // PAGE_SIZE
    own_k, own_v = k_new, v_new
    if n_own_pages > 0:
        pages = page_table[:, n_full * ppb:]
        pk = ck[pages].reshape(B, n_own_pages * PAGE_SIZE, N_KV_HEADS, HEAD_DIM)
        pv = cv[pages].reshape(B, n_own_pages * PAGE_SIZE, N_KV_HEADS, HEAD_DIM)
        own_k = jnp.concatenate([pk, k_new], axis=1)
        own_v = jnp.concatenate([pv, v_new], axis=1)
    own_kpos = own_start + jnp.arange(own_k.shape[1])
    k_sel = min(MOBA_TOPK, n_full)
    if k_sel == 0:
        return moba_attend(q, PAST_LEN + jnp.arange(T), None, None, None, own_k, own_v, own_kpos)
    page_sum = jnp.sum(ck.astype(jnp.float32), axis=1)
    kmean = page_sum[page_table[:, :n_full * ppb]].reshape(B, n_full, ppb, N_KV_HEADS, HEAD_DIM).sum(axis=2) / MOBA_BLOCK
    idx, valid = moba_select(q, kmean, jnp.full((T,), n_full, jnp.int32), k_sel)
    bi = jnp.arange(B)[:, None, None, None, None]
    phys = page_table[bi, idx[..., None] * ppb + jnp.arange(ppb)]
    hk = (jnp.arange(N_Q_HEADS) // KV_GROUP)[None, None, :, None, None]

    def tok(t):
        ph = lax.dynamic_index_in_dim(phys, t, axis=1, keepdims=True)
        sel_k = ck[ph, :, hk].reshape(B, 1, N_Q_HEADS, k_sel * MOBA_BLOCK, HEAD_DIM)
        sel_v = cv[ph, :, hk].reshape(B, 1, N_Q_HEADS, k_sel * MOBA_BLOCK, HEAD_DIM)
        sel_valid = jnp.repeat(lax.dynamic_index_in_dim(valid, t, axis=1, keepdims=True), MOBA_BLOCK, axis=-1)
        qt = lax.dynamic_slice_in_dim(q, t, 1, axis=1)
        qpos = jnp.reshape(PAST_LEN + t, (1,))
        return moba_attend(qt, qpos, sel_k, sel_v, sel_valid, own_k, own_v, own_kpos)

    out = lax.map(tok, jnp.arange(T))
    return out.transpose(1, 0, 2, 3).reshape(B, T, ATTN_WIDTH)


def setup_inputs(seed: int = 0) -> dict:
    key = jax.random.key(seed)
    ks = jax.random.split(key, 16)
    n_pages = PAST_LEN // PAGE_SIZE
    n_used = DEC_BATCH * n_pages
    n_pool = n_used + max(1, n_used // 4)
    f32 = jnp.float32
    nrm = lambda k, shape, s: jax.random.normal(k, shape, f32) * s
    page_table = jax.random.permutation(ks[5], n_pool)[:n_used].reshape(DEC_BATCH, n_pages).astype(jnp.int32)
    return {
        "x_prompt": nrm(ks[0], (BATCH, SEQ, D_MODEL), 1.0),
        "x_sample": nrm(ks[1], (DEC_BATCH, DEC_SEQ, D_MODEL), 1.0),
        "cache_k": nrm(ks[2], (DEPTH, n_pool, PAGE_SIZE, N_KV_HEADS, HEAD_DIM), 1.0),
        "cache_v": nrm(ks[3], (DEPTH, n_pool, PAGE_SIZE, N_KV_HEADS, HEAD_DIM), 1.0),
        "state_h": nrm(ks[4], (DEPTH, DEC_BATCH, HG_HEADS, HG_KDIM, HG_VDIM), 0.5),
        "page_table": page_table,
        "norm1_g": 1.0 + nrm(ks[6], (DEPTH, D_MODEL), 0.02),
        "w_in": nrm(ks[7], (DEPTH, D_MODEL, IN_WIDTH), D_MODEL ** -0.5),
        "q_norm_g": 1.0 + nrm(ks[8], (DEPTH, HEAD_DIM), 0.02),
        "k_norm_g": 1.0 + nrm(ks[9], (DEPTH, HEAD_DIM), 0.02),
        "lb_logits": nrm(ks[10], (DEPTH + 1, HG_KWIDTH), 0.5),
        "hg_norm_g": 1.0 + nrm(ks[11], (DEPTH, HG_WIDTH), 0.02),
        "w_out": nrm(ks[12], (DEPTH, MIX_WIDTH, D_MODEL), MIX_WIDTH ** -0.5),
        "norm2_g": 1.0 + nrm(ks[13], (DEPTH, D_MODEL), 0.02),
        "w_up": nrm(ks[14], (DEPTH, D_MODEL, D_FF), D_MODEL ** -0.5),
        "w_down": nrm(ks[15], (DEPTH, D_FF, D_MODEL), D_FF ** -0.5),
    }


def reference(x_prompt, x_sample, cache_k, cache_v, state_h, page_table, norm1_g, w_in, q_norm_g, k_norm_g,
              lb_logits, hg_norm_g, w_out, norm2_g, w_up, w_down):
    lb_all = jnp.cumsum(jax.nn.softmax(lb_logits.astype(jnp.float32), axis=0), axis=0)[:DEPTH]
    x_p, x_s = x_prompt, x_sample
    kp_l, vp_l, hp_l, ks_l, vs_l, hs_l = [], [], [], [], [], []
    for l in range(DEPTH):
        q, k, v, hq, hk, hv, hlf, hg = mixer_inputs(x_p, norm1_g[l], w_in[l], q_norm_g[l], k_norm_g[l], lb_all[l])
        attn_o = moba_prompt(q, k, v)
        h0 = jnp.zeros((x_p.shape[0], HG_HEADS, HG_KDIM, HG_VDIM), jnp.float32)
        hg_o, h_T = hgrn2_scan(hq, hk, hv, hlf, h0)
        x_p = mixer_output(x_p, attn_o, hg_o, hg, hg_norm_g[l], w_out[l], norm2_g[l], w_up[l], w_down[l])
        kp_l.append(k); vp_l.append(v); hp_l.append(h_T.astype(x_prompt.dtype))
        q, k, v, hq, hk, hv, hlf, hg = mixer_inputs(x_s, norm1_g[l], w_in[l], q_norm_g[l], k_norm_g[l], lb_all[l])
        attn_o = moba_sample(q, k, v, cache_k[l], cache_v[l], page_table)
        hg_o, h_T = hgrn2_scan(hq, hk, hv, hlf, state_h[l])
        x_s = mixer_output(x_s, attn_o, hg_o, hg, hg_norm_g[l], w_out[l], norm2_g[l], w_up[l], w_down[l])
        ks_l.append(k); vs_l.append(v); hs_l.append(h_T.astype(state_h.dtype))
    k_prompt, v_prompt, h_prompt = jnp.stack(kp_l), jnp.stack(vp_l), jnp.stack(hp_l)
    k_sample, v_sample, h_sample = jnp.stack(ks_l), jnp.stack(vs_l), jnp.stack(hs_l)
    return (x_p, x_s, k_prompt, v_prompt, h_prompt, k_sample, v_sample, h_sample)
```

```python
import functools

import jax
import jax.numpy as jnp
from jax import lax
from jax.experimental import pallas as pl
from jax.experimental.pallas import tpu as pltpu

F32 = jnp.float32
BF16 = jnp.bfloat16

HEAD_DIM = 128
KV_GROUP = 2
PAGE_SIZE = 128
MOBA_BLOCK = 256
MOBA_TOPK = 3
HG_CHUNK = 64
HG_SUB = 16
EPS = 1e-6
NEG = -1e30
LANES = 128
VMEM_LIMIT = 56 * 1024 * 1024


def _tile(n, target, mult=1):
    for t in range(min(n, target), 0, -1):
        if n % t == 0 and t % mult == 0:
            return t
    return n


def _params(sem):
    return pltpu.CompilerParams(dimension_semantics=sem, vmem_limit_bytes=VMEM_LIMIT)


def _nt_dot(a, b):
    return lax.dot_general(a, b, (((1,), (1,)), ((), ())), preferred_element_type=F32)


def _tn_dot(a, b):
    return lax.dot_general(a, b, (((0,), (0,)), ((), ())), preferred_element_type=F32)


def _rmsnorm_body(x_ref, g_ref, o_ref):
    x = x_ref[...]
    ms = jnp.mean(x * x, axis=-1, keepdims=True)
    o_ref[...] = (x * lax.rsqrt(ms + EPS) * g_ref[...]).astype(o_ref.dtype)


def _rmsnorm(x, g, tm):
    m, d = x.shape
    return pl.pallas_call(
        _rmsnorm_body,
        grid=(m // tm,),
        in_specs=[pl.BlockSpec((tm, d), lambda i: (i, 0)), pl.BlockSpec((1, d), lambda i: (0, 0))],
        out_specs=pl.BlockSpec((tm, d), lambda i: (i, 0)),
        out_shape=jax.ShapeDtypeStruct((m, d), BF16),
        compiler_params=_params(("arbitrary",)),
        name="rmsnorm",
    )(x, g.reshape(1, d))


def _matmul_body(*refs, n_pairs, n_rows, n_cols, n_outs, nk, epilogue):
    xs = refs[0:2 * n_pairs:2]
    ws = refs[1:2 * n_pairs:2]
    p = 2 * n_pairs
    row_refs = refs[p:p + n_rows]
    col_refs = refs[p + n_rows:p + n_rows + n_cols]
    out_refs = refs[p + n_rows + n_cols:p + n_rows + n_cols + n_outs]
    part = None
    for x_ref, w_ref in zip(xs, ws):
        d = jnp.dot(x_ref[...], w_ref[...], preferred_element_type=F32)
        part = d if part is None else part + d
    if nk == 1:
        epilogue(part, row_refs, col_refs, out_refs)
        return
    acc_ref = refs[-1]
    k = pl.program_id(2)

    @pl.when(k == 0)
    def _():
        acc_ref[...] = part

    @pl.when(k > 0)
    def _():
        acc_ref[...] += part

    @pl.when(k == nk - 1)
    def _():
        epilogue(acc_ref[...], row_refs, col_refs, out_refs)


def _matmul(name, pairs, *, n, col_off, tm, tn, nk, epilogue, rows=(), cols=(), outs=()):
    m = pairs[0][0].shape[0]
    assert m % tm == 0 and n % tn == 0 and col_off % tn == 0
    joff = col_off // tn
    in_specs, args = [], []
    for x, w in pairs:
        kdim = x.shape[1]
        assert kdim % nk == 0 and w.shape[0] == kdim
        tk = kdim // nk
        in_specs += [pl.BlockSpec((tm, tk), lambda i, j, k: (i, k)),
                     pl.BlockSpec((tk, tn), lambda i, j, k: (k, j + joff))]
        args += [x, w]
    for r in rows:
        in_specs.append(pl.BlockSpec((tm, tn), lambda i, j, k: (i, j)))
        args.append(r)
    for c in cols:
        in_specs.append(pl.BlockSpec((c.shape[0], tn), lambda i, j, k: (0, j)))
        args.append(c)
    out_specs, out_shape = [], []
    for o in outs:
        if isinstance(o, tuple):
            out_specs.append(pl.BlockSpec((tm, tn // HEAD_DIM, HEAD_DIM), lambda i, j, k: (i, j, 0)))
            out_shape.append(jax.ShapeDtypeStruct((m, n // HEAD_DIM, HEAD_DIM), o[0]))
        else:
            out_specs.append(pl.BlockSpec((tm, tn), lambda i, j, k: (i, j)))
            out_shape.append(jax.ShapeDtypeStruct((m, n), o))
    body = functools.partial(_matmul_body, n_pairs=len(pairs), n_rows=len(rows), n_cols=len(cols),
                             n_outs=len(outs), nk=nk, epilogue=epilogue)
    return pl.pallas_call(
        body,
        grid=(m // tm, n // tn, nk),
        in_specs=in_specs,
        out_specs=out_specs,
        out_shape=out_shape,
        scratch_shapes=[pltpu.VMEM((tm, tn), F32)] if nk > 1 else [],
        compiler_params=_params(("arbitrary", "arbitrary", "arbitrary")),
        name=name,
    )(*args)


def _store_head(o, h, y):
    if len(o.shape) == 3:
        o[:, h, :] = y.astype(o.dtype)
    else:
        o[:, h * HEAD_DIM:(h + 1) * HEAD_DIM] = y.astype(o.dtype)


def _epi_head_norm(acc, rows, cols, outs):
    g = cols[0][...]
    for h in range(acc.shape[1] // HEAD_DIM):
        sl = slice(h * HEAD_DIM, (h + 1) * HEAD_DIM)
        blk = acc[:, sl]
        y = blk * lax.rsqrt(jnp.mean(blk * blk, axis=-1, keepdims=True) + EPS) * g[:, sl]
        for o in outs:
            _store_head(o, h, y)


def _epi_copy(acc, rows, cols, outs):
    for h in range(acc.shape[1] // HEAD_DIM):
        for o in outs:
            _store_head(o, h, acc[:, h * HEAD_DIM:(h + 1) * HEAD_DIM])


def _epi_silu(acc, rows, cols, outs):
    outs[0][...] = (acc * jax.nn.sigmoid(acc)).astype(outs[0].dtype)


def _epi_sigmoid(acc, rows, cols, outs):
    outs[0][...] = jax.nn.sigmoid(acc).astype(outs[0].dtype)


def _epi_forget(acc, rows, cols, outs, *, layer):
    logits = cols[0][...]
    e = jnp.exp(logits - jnp.max(logits, axis=0, keepdims=True))
    lb = jnp.sum(e[:layer + 1], axis=0, keepdims=True) / jnp.sum(e, axis=0, keepdims=True)
    forget = lb + (1.0 - lb) * jax.nn.sigmoid(acc)
    outs[0][...] = jnp.log(forget)
    outs[1][...] = 1.0 - forget


def _epi_residual(acc, rows, cols, outs):
    outs[0][...] = rows[0][...] + acc


def _epi_relu2(acc, rows, cols, outs):
    r = jnp.maximum(acc, 0.0)
    outs[0][...] = (r * r).astype(outs[0].dtype)


def _select_blocks_t(sc, n_valid, topk):
    nb = sc.shape[0]
    row = lax.broadcasted_iota(jnp.int32, sc.shape, 0)
    valid = row < n_valid
    sel = jnp.zeros(sc.shape, F32)
    for j in range(nb):
        sj = sc[j:j + 1, :]
        beats = ((sc > sj) | ((sc == sj) & (row < j))) & valid
        rank = jnp.sum(jnp.where(beats, 1.0, 0.0), axis=0, keepdims=True)
        chosen = jnp.where((rank < float(topk)) & (j < n_valid), 1.0, 0.0)
        sel = jnp.where(row == j, chosen, sel)
    return sel


def _moba_prompt_body(q_ref, k_ref, v_ref, o_ref, kmean_sc, m_sc, l_sc, acc_sc, *, nb, scale):
    i = pl.program_id(2)
    blk = MOBA_BLOCK
    rows = KV_GROUP * blk

    @pl.when(i == 0)
    def _():
        kall = k_ref[...].astype(F32)
        kmean_sc[...] = jnp.sum(kall.reshape(nb, blk, HEAD_DIM), axis=1) * (1.0 / blk)

    qb = q_ref[...]
    q2 = jnp.concatenate([qb[:, g * HEAD_DIM:(g + 1) * HEAD_DIM] for g in range(KV_GROUP)], axis=0)

    sc_t = _nt_dot(kmean_sc[...].astype(BF16), q2)
    sel_t = _select_blocks_t(sc_t, i, MOBA_TOPK)
    brow = lax.broadcasted_iota(jnp.int32, sel_t.shape, 0)
    bias_t = jnp.where((sel_t > 0.5) | (brow == i), 0.0, NEG)
    pad = jnp.zeros((LANES - nb, rows), F32)
    bias = jnp.concatenate([bias_t, pad], axis=0).T
    qx = jnp.concatenate([q2, bias.astype(BF16)], axis=1)
    lane = lax.broadcasted_iota(jnp.int32, (blk, LANES), 1)

    def scores(j):
        r0 = pl.multiple_of(j * blk, blk)
        kj = k_ref[pl.ds(r0, blk), :]
        onehot = jnp.where(lane == j, 1.0, 0.0).astype(BF16)
        return _nt_dot(qx, jnp.concatenate([kj, onehot], axis=1)), r0

    s, r0 = scores(i)
    qpos = lax.broadcasted_iota(jnp.int32, s.shape, 0) % blk
    kpos = lax.broadcasted_iota(jnp.int32, s.shape, 1)
    s = jnp.where(kpos <= qpos, s, NEG)
    m0 = jnp.max(s, axis=-1, keepdims=True)
    p = jnp.exp((s - m0) * scale)
    m_sc[...] = jnp.broadcast_to(m0, m_sc.shape)
    l_sc[...] = jnp.broadcast_to(jnp.sum(p, axis=-1, keepdims=True), l_sc.shape)
    acc_sc[...] = jnp.dot(p.astype(BF16), v_ref[pl.ds(r0, blk), :], preferred_element_type=F32)

    def past(j, carry):
        s, r0 = scores(j)
        m_prev = m_sc[...]
        m_new = jnp.maximum(m_prev, jnp.max(s, axis=-1, keepdims=True))
        alpha = jnp.exp((m_prev - m_new) * scale)
        p = jnp.exp((s - m_new[:, :1]) * scale)
        l_sc[...] = alpha * l_sc[...] + jnp.sum(p, axis=-1, keepdims=True)
        acc_sc[...] = alpha * acc_sc[...] + jnp.dot(p.astype(BF16), v_ref[pl.ds(r0, blk), :],
                                                    preferred_element_type=F32)
        m_sc[...] = m_new
        return carry

    lax.fori_loop(0, i, past, 0)

    o = acc_sc[...] / l_sc[...]
    for g in range(KV_GROUP):
        o_ref[:, g * HEAD_DIM:(g + 1) * HEAD_DIM] = o[g * blk:(g + 1) * blk].astype(o_ref.dtype)


def _moba_prompt(q, kb, vb, *, batch, seq):
    assert seq % MOBA_BLOCK == 0
    nb = seq // MOBA_BLOCK
    assert nb <= LANES
    n_kv = kb.shape[1] // HEAD_DIM
    gw = KV_GROUP * HEAD_DIM
    rows = KV_GROUP * MOBA_BLOCK
    body = functools.partial(_moba_prompt_body, nb=nb, scale=HEAD_DIM ** -0.5)
    return pl.pallas_call(
        body,
        grid=(batch, n_kv, nb),
        in_specs=[pl.BlockSpec((MOBA_BLOCK, gw), lambda b, n, i: (b * nb + i, n)),
                  pl.BlockSpec((seq, HEAD_DIM), lambda b, n, i: (b, n)),
                  pl.BlockSpec((seq, HEAD_DIM), lambda b, n, i: (b, n))],
        out_specs=pl.BlockSpec((MOBA_BLOCK, gw), lambda b, n, i: (b * nb + i, n)),
        out_shape=jax.ShapeDtypeStruct(q.shape, BF16),
        scratch_shapes=[pltpu.VMEM((nb, HEAD_DIM), F32), pltpu.VMEM((rows, LANES), F32),
                        pltpu.VMEM((rows, LANES), F32), pltpu.VMEM((rows, HEAD_DIM), F32)],
        compiler_params=_params(("arbitrary", "arbitrary", "arbitrary")),
        name="moba_prompt",
    )(q, kb, vb)


def _moba_sample_body(pt_ref, q_ref, kn_ref, vn_ref, *rest, pps, nch, n_kv, t_new, n_pages, scale):
    k_refs = rest[:pps]
    v_refs = rest[pps:2 * pps]
    o_ref = rest[2 * pps]
    qbd_sc, s_sc, p_sc, km_sc, oacc_sc, l_sc = rest[2 * pps + 1:]
    c = pl.program_id(1)
    ppb = MOBA_BLOCK // PAGE_SIZE
    nb = n_pages // ppb
    gt = KV_GROUP * t_new
    rows = n_kv * gt
    kvw = n_kv * HEAD_DIM

    @pl.when(c == 0)
    def _():
        qv = q_ref[...].astype(F32)
        per_g = [jnp.concatenate([qv[:, (KV_GROUP * n + g) * HEAD_DIM:(KV_GROUP * n + g + 1) * HEAD_DIM]
                                  for n in range(n_kv)], axis=1) for g in range(KV_GROUP)]
        tiled = jnp.concatenate([per_g[g] for n in range(n_kv) for g in range(KV_GROUP)], axis=0)
        rown = lax.broadcasted_iota(jnp.int32, (rows, kvw), 0) // gt
        coln = lax.broadcasted_iota(jnp.int32, (rows, kvw), 1) // HEAD_DIM
        qbd_sc[...] = jnp.where(rown == coln, tiled, 0.0).astype(BF16)

    def page_rows(ref):
        return jnp.concatenate([ref[:, n, :] for n in range(n_kv)], axis=1)

    @pl.when(c < nch)
    def _():
        qbd = qbd_sc[...]
        colsum = None
        for i in range(pps):
            cs = jnp.sum(k_refs[i][...], axis=0)
            colsum = cs if i % ppb == 0 else colsum + cs
            if i % ppb == ppb - 1:
                km_sc[pl.ds(c * (pps // ppb) + i // ppb, 1)] = colsum[None]
            col0 = pl.multiple_of((c * pps + i) * PAGE_SIZE, PAGE_SIZE)
            s_sc[:, pl.ds(col0, PAGE_SIZE)] = _nt_dot(qbd, page_rows(k_refs[i]).astype(BF16))

    @pl.when(c == nch - 1)
    def _():
        qbd = qbd_sc[...]
        kmean = (page_rows(km_sc) * (1.0 / MOBA_BLOCK)).astype(BF16)
        sc_t = _nt_dot(kmean, qbd)
        sel_t = _select_blocks_t(sc_t, nb, MOBA_TOPK)
        bias = jnp.where(sel_t > 0.5, 0.0, NEG).T
        zpad = jnp.zeros((LANES - t_new, kvw), F32)
        knp = jnp.concatenate([kn_ref[...].astype(F32), zpad], axis=0).astype(BF16)
        vnp = jnp.concatenate([vn_ref[...].astype(F32), zpad], axis=0).astype(BF16)
        s_own = _nt_dot(qbd, knp)
        tok = lax.broadcasted_iota(jnp.int32, s_own.shape, 0) % t_new
        key = lax.broadcasted_iota(jnp.int32, s_own.shape, 1)
        s_own = jnp.where(key <= tok, s_own, NEG)
        m = jnp.max(s_own, axis=-1, keepdims=True)
        for j in range(nb):
            sl = slice(j * MOBA_BLOCK, (j + 1) * MOBA_BLOCK)
            sb = s_sc[:, sl] + bias[:, j:j + 1]
            s_sc[:, sl] = sb
            m = jnp.maximum(m, jnp.max(sb, axis=-1, keepdims=True))
        p_own = jnp.exp((s_own - m) * scale)
        l = jnp.sum(p_own, axis=-1, keepdims=True)
        for j in range(nb):
            sl = slice(j * MOBA_BLOCK, (j + 1) * MOBA_BLOCK)
            p = jnp.exp((s_sc[:, sl] - m) * scale)
            l = l + jnp.sum(p, axis=-1, keepdims=True)
            p_sc[:, sl] = p.astype(BF16)
        l_sc[...] = jnp.broadcast_to(l, l_sc.shape)
        oacc_sc[...] = jnp.dot(p_own.astype(BF16), vnp, preferred_element_type=F32)

    @pl.when(c >= nch)
    def _():
        acc = oacc_sc[...]
        for i in range(pps):
            col0 = pl.multiple_of(((c - nch) * pps + i) * PAGE_SIZE, PAGE_SIZE)
            acc = acc + jnp.dot(p_sc[:, pl.ds(col0, PAGE_SIZE)], page_rows(v_refs[i]).astype(BF16),
                                preferred_element_type=F32)
        oacc_sc[...] = acc

    @pl.when(c == 2 * nch - 1)
    def _():
        o = oacc_sc[...] / l_sc[:, :1]
        for n in range(n_kv):
            for g in range(KV_GROUP):
                r0 = n * gt + g * t_new
                h = KV_GROUP * n + g
                o_ref[:, h * HEAD_DIM:(h + 1) * HEAD_DIM] = (
                    o[r0:r0 + t_new, n * HEAD_DIM:(n + 1) * HEAD_DIM].astype(o_ref.dtype))


def _moba_sample(q, kn, vn, cache_k, cache_v, page_table, *, batch, t_new):
    n_pages = page_table.shape[1]
    ppb = MOBA_BLOCK // PAGE_SIZE
    assert n_pages % ppb == 0 and n_pages // ppb >= MOBA_TOPK
    kvw = kn.shape[1]
    n_kv = kvw // HEAD_DIM
    pps = _tile(n_pages, 8, ppb)
    nch = n_pages // pps
    rows = n_kv * KV_GROUP * t_new
    past = n_pages * PAGE_SIZE

    def k_map(i):
        return lambda b, c, pt: (pt[b, jnp.minimum(c, nch - 1) * pps + i], 0, 0, 0)

    def v_map(i):
        return lambda b, c, pt: (pt[b, jnp.maximum(c - nch, 0) * pps + i], 0, 0, 0)

    in_specs = [pl.BlockSpec((t_new, q.shape[1]), lambda b, c, pt: (b, 0)),
                pl.BlockSpec((t_new, kvw), lambda b, c, pt: (b, 0)),
                pl.BlockSpec((t_new, kvw), lambda b, c, pt: (b, 0))]
    in_specs += [pl.BlockSpec((None, PAGE_SIZE, n_kv, HEAD_DIM), k_map(i)) for i in range(pps)]
    in_specs += [pl.BlockSpec((None, PAGE_SIZE, n_kv, HEAD_DIM), v_map(i)) for i in range(pps)]
    body = functools.partial(_moba_sample_body, pps=pps, nch=nch, n_kv=n_kv, t_new=t_new,
                             n_pages=n_pages, scale=HEAD_DIM ** -0.5)
    grid_spec = pltpu.PrefetchScalarGridSpec(
        num_scalar_prefetch=1,
        grid=(batch, 2 * nch),
        in_specs=in_specs,
        out_specs=pl.BlockSpec((t_new, q.shape[1]), lambda b, c, pt: (b, 0)),
        scratch_shapes=[pltpu.VMEM((rows, kvw), BF16),
                        pltpu.VMEM((rows, past), F32),
                        pltpu.VMEM((rows, past), BF16),
                        pltpu.VMEM((n_pages // ppb, n_kv, HEAD_DIM), F32),
                        pltpu.VMEM((rows, kvw), F32),
                        pltpu.VMEM((rows, LANES), F32)],
    )
    return pl.pallas_call(
        body,
        grid_spec=grid_spec,
        out_shape=jax.ShapeDtypeStruct(q.shape, BF16),
        compiler_params=_params(("arbitrary", "arbitrary")),
        name="moba_sample",
    )(page_table, q, kn, vn, *([cache_k] * pps), *([cache_v] * pps))


def _hgrn_body(q_ref, lf_ref, k_ref, v_ref, gate_ref, gn_ref, h0_ref, o_ref, hout_ref, ht_sc,
               *, hb, chunk, sub, n_chunks, n_tsteps):
    t = pl.program_id(2)
    ns = chunk // sub

    @pl.when(t == 0)
    def _():
        for h in range(hb):
            ht_sc[h] = h0_ref[h].T

    r_i = lax.broadcasted_iota(jnp.int32, (chunk, chunk), 0)
    c_i = lax.broadcasted_iota(jnp.int32, (chunk, chunk), 1)
    causal = c_i <= r_i
    tri = jnp.where(causal, 1.0, 0.0).astype(BF16)
    rowc = lax.broadcasted_iota(jnp.int32, (chunk, HEAD_DIM), 0)

    def one_chunk(ci, carry):
        r0 = pl.multiple_of(ci * chunk, chunk)
        for h in range(hb):
            cs = slice(h * HEAD_DIM, (h + 1) * HEAD_DIM)
            lf = lf_ref[pl.ds(r0, chunk), cs]
            q = q_ref[pl.ds(r0, chunk), cs]
            k = k_ref[pl.ds(r0, chunk), cs]
            v = v_ref[pl.ds(r0, chunk), cs]
            hi = lf.astype(BF16)
            lo = (lf - hi.astype(F32)).astype(BF16)
            bb = jnp.dot(tri, jnp.concatenate([hi, lo], axis=1), preferred_element_type=F32)
            b = bb[:, :HEAD_DIM] + bb[:, HEAD_DIM:]
            b_last = b[chunk - 1:chunk, :]
            ht = ht_sc[h]
            qe = (q * jnp.exp(b)).astype(BF16)
            o = _nt_dot(qe, ht.astype(BF16))
            refs = [jnp.zeros((1, HEAD_DIM), F32)] + [b[sub * s - 1:sub * s, :] for s in range(1, ns)]
            gfull = jnp.concatenate([jnp.broadcast_to(r, (sub, HEAD_DIM)) for r in refs], axis=0)
            qh = q * jnp.exp(b - gfull)
            qcat = jnp.concatenate(
                [jnp.where((rowc >= sub * s) & (rowc < sub * (s + 1)), qh, 0.0) for s in range(ns)],
                axis=1).astype(BF16)
            kcat = jnp.concatenate(
                [jnp.where(rowc < sub * (s + 1), k * jnp.exp(refs[s] - b), 0.0) for s in range(ns)],
                axis=1).astype(BF16)
            att = jnp.where(causal, _nt_dot(qcat, kcat), 0.0)
            o = o + jnp.dot(att.astype(BF16), v, preferred_element_type=F32)
            kd = (k * jnp.exp(b_last - b)).astype(BF16)
            ht_sc[h] = ht * jnp.exp(b_last) + _tn_dot(v, kd)
            y = o * lax.rsqrt(jnp.mean(o * o, axis=-1, keepdims=True) + EPS) * gn_ref[:, cs]
            o_ref[pl.ds(r0, chunk), cs] = (y * gate_ref[pl.ds(r0, chunk), cs]).astype(o_ref.dtype)
        return carry

    lax.fori_loop(0, n_chunks, one_chunk, 0)

    @pl.when(t == n_tsteps - 1)
    def _():
        for h in range(hb):
            hout_ref[h] = ht_sc[h].T


def _hgrn(hq, lf, hk, hv, gate, gnorm, h0, *, batch, seq):
    width = hq.shape[1]
    n_heads = width // HEAD_DIM
    hb = _tile(n_heads, 4)
    chunk = HG_CHUNK if seq % HG_CHUNK == 0 else seq
    sub = HG_SUB if chunk % HG_SUB == 0 else chunk
    tc = _tile(seq, 512, chunk)
    nt = seq // tc
    bw = hb * HEAD_DIM
    row_spec = pl.BlockSpec((tc, bw), lambda b, g, t: (b * nt + t, g))
    st_spec = pl.BlockSpec((None, hb, HEAD_DIM, HEAD_DIM), lambda b, g, t: (b, g, 0, 0))
    body = functools.partial(_hgrn_body, hb=hb, chunk=chunk, sub=sub, n_chunks=tc // chunk, n_tsteps=nt)
    return pl.pallas_call(
        body,
        grid=(batch, n_heads // hb, nt),
        in_specs=[row_spec, row_spec, row_spec, row_spec, row_spec,
                  pl.BlockSpec((1, bw), lambda b, g, t: (0, g)), st_spec],
        out_specs=[row_spec, st_spec],
        out_shape=[jax.ShapeDtypeStruct(hq.shape, BF16), jax.ShapeDtypeStruct(h0.shape, F32)],
        scratch_shapes=[pltpu.VMEM((hb, HEAD_DIM, HEAD_DIM), F32)],
        compiler_params=_params(("arbitrary", "arbitrary", "arbitrary")),
        name="hgrn",
    )(hq, lf, hk, hv, gate, gnorm, h0)


def _group_inputs(x, w, *, tm):
    d = x.shape[1]
    attn_w = d // 2
    kv_w = attn_w // KV_GROUP
    hg_w = d - attn_w
    xn = _rmsnorm(x, w["norm1_g"], _tile(x.shape[0], 256, 8))
    starts = [0, attn_w, attn_w + kv_w, attn_w + 2 * kv_w, attn_w + 2 * kv_w + hg_w,
              attn_w + 2 * kv_w + 2 * hg_w, attn_w + 2 * kv_w + 3 * hg_w]

    def proj(name, seg, n, epilogue, cols=(), outs=(), whole=False):
        tn = n if whole else _tile(n, 512, LANES)
        while starts[seg] % tn:
            tn = _tile(n, tn - 1, LANES)
        assert tn == n or not whole
        tm_call = _tile(xn.shape[0], min(tm, 512), 8) if whole else tm
        return _matmul(name, [(xn, w["w_in"])], n=n, col_off=starts[seg], tm=tm_call, tn=tn, nk=1,
                       epilogue=epilogue, cols=cols, outs=outs)

    (q,) = proj("proj_q", 0, attn_w, _epi_head_norm, cols=(w["q_gain"],), outs=(BF16,))
    k, kb = proj("proj_k", 1, kv_w, _epi_head_norm, cols=(w["k_gain"],), outs=((F32, "heads"), BF16),
                 whole=True)
    v, vb = proj("proj_v", 2, kv_w, _epi_copy, outs=((F32, "heads"), BF16), whole=True)
    (hq,) = proj("proj_hq", 3, hg_w, _epi_silu, outs=(F32,))
    lf, hk = proj("proj_hf", 4, hg_w, functools.partial(_epi_forget, layer=0),
                  cols=(w["lb_logits"],), outs=(F32, F32))
    (hv,) = proj("proj_hi", 5, hg_w, _epi_copy, outs=(BF16,))
    (gate,) = proj("proj_hg", 6, hg_w, _epi_sigmoid, outs=(F32,))
    return dict(q=q, k=k, kb=kb, v=v, vb=vb, hq=hq, lf=lf, hk=hk, hv=hv, gate=gate)


def _group_outputs(x, attn_o, hg_o, w, *, tm):
    m, d = x.shape
    d_ff = w["w_up"].shape[1]
    tn = _tile(d, 512, LANES)
    (x1,) = _matmul("proj_out", [(attn_o, w["w_out_a"]), (hg_o, w["w_out_h"])], n=d, col_off=0, tm=tm,
                    tn=tn, nk=1, epilogue=_epi_residual, rows=(x,), outs=(F32,))
    hn = _rmsnorm(x1, w["norm2_g"], _tile(m, 256, 8))
    (act,) = _matmul("mlp_up", [(hn, w["w_up"])], n=d_ff, col_off=0, tm=tm, tn=_tile(d_ff, 1024, LANES),
                     nk=1, epilogue=_epi_relu2, outs=(BF16,))
    nk = d_ff // _tile(d_ff, 2048, LANES)
    (y,) = _matmul("mlp_down", [(act, w["w_down"])], n=d, col_off=0, tm=tm, tn=_tile(d, 1024, LANES),
                   nk=nk, epilogue=_epi_residual, rows=(x1,), outs=(F32,))
    return y


def kernel(x_prompt, x_sample, cache_k, cache_v, state_h, page_table, norm1_g, w_in, q_norm_g, k_norm_g,
           lb_logits, hg_norm_g, w_out, norm2_g, w_up, w_down):
    depth = w_in.shape[0]
    assert depth == 1, "one layer"
    bp, tp, d = x_prompt.shape
    bs, ts, _ = x_sample.shape
    attn_w = d // 2
    n_q = attn_w // HEAD_DIM
    n_kv = n_q // KV_GROUP
    kv_w = n_kv * HEAD_DIM
    hg_w = d - attn_w
    n_hg = hg_w // HEAD_DIM

    w = dict(
        norm1_g=norm1_g[0], norm2_g=norm2_g[0],
        w_in=w_in[0].astype(BF16),
        w_out_a=w_out[0, :attn_w].astype(BF16), w_out_h=w_out[0, attn_w:].astype(BF16),
        w_up=w_up[0].astype(BF16), w_down=w_down[0].astype(BF16),
        q_gain=jnp.tile(q_norm_g[0], n_q).reshape(1, attn_w),
        k_gain=jnp.tile(k_norm_g[0], n_kv).reshape(1, kv_w),
        lb_logits=lb_logits,
    )
    gnorm = hg_norm_g[0].reshape(1, hg_w)

    xp = x_prompt.reshape(bp * tp, d)
    tm_p = _tile(bp * tp, 1024, 8)
    gp = _group_inputs(xp, w, tm=tm_p)
    attn_p = _moba_prompt(gp["q"], gp["kb"], gp["vb"], batch=bp, seq=tp)
    h0_p = jnp.zeros((bp, n_hg, HEAD_DIM, HEAD_DIM), F32)
    hg_p, h_p = _hgrn(gp["hq"], gp["lf"], gp["hk"], gp["hv"], gp["gate"], gnorm, h0_p, batch=bp, seq=tp)
    y_p = _group_outputs(xp, attn_p, hg_p, w, tm=tm_p)

    xs = x_sample.reshape(bs * ts, d)
    tm_s = _tile(bs * ts, 256, 8)
    gs = _group_inputs(xs, w, tm=tm_s)
    attn_s = _moba_sample(gs["q"], gs["kb"], gs["vb"], cache_k[0], cache_v[0], page_table, batch=bs, t_new=ts)
    hg_s, h_s = _hgrn(gs["hq"], gs["lf"], gs["hk"], gs["hv"], gs["gate"], gnorm, state_h[0], batch=bs, seq=ts)
    y_s = _group_outputs(xs, attn_s, hg_s, w, tm=tm_s)

    return (y_p.reshape(bp, tp, d), y_s.reshape(bs, ts, d),
            gp["k"].reshape(1, bp, tp, n_kv, HEAD_DIM), gp["v"].reshape(1, bp, tp, n_kv, HEAD_DIM),
            h_p.reshape(1, bp, n_hg, HEAD_DIM, HEAD_DIM),
            gs["k"].reshape(1, bs, ts, n_kv, HEAD_DIM), gs["v"].reshape(1, bs, ts, n_kv, HEAD_DIM),
            h_s.reshape(1, bs, n_hg, HEAD_DIM, HEAD_DIM))
```

```python
import functools

import jax
import jax.numpy as jnp
from jax import lax
from jax.experimental import pallas as pl
from jax.experimental.pallas import tpu as pltpu

F32 = jnp.float32
BF16 = jnp.bfloat16

HEAD_DIM = 128
KV_GROUP = 2
PAGE_SIZE = 128
MOBA_BLOCK = 256
MOBA_TOPK = 3
HG_CHUNK = 64
HG_SUB = 16
EPS = 1e-6
NEG = -1e30
LANES = 128
SUB_N = 512
VMEM_LIMIT = 56 * 1024 * 1024


def _tile(n, target, mult=1):
    for t in range(min(n, target), 0, -1):
        if n % t == 0 and t % mult == 0:
            return t
    return n


def _params(sem):
    return pltpu.CompilerParams(dimension_semantics=sem, vmem_limit_bytes=VMEM_LIMIT)


def _nt_dot(a, b):
    return lax.dot_general(a, b, (((1,), (1,)), ((), ())), preferred_element_type=F32)


def _tn_dot(a, b):
    return lax.dot_general(a, b, (((0,), (0,)), ((), ())), preferred_element_type=F32)


def _rmsnorm_body(x_ref, g_ref, o_ref):
    x = x_ref[...]
    ms = jnp.mean(x * x, axis=-1, keepdims=True)
    o_ref[...] = (x * lax.rsqrt(ms + EPS) * g_ref[...]).astype(o_ref.dtype)


def _rmsnorm(x, g, tm):
    m, d = x.shape
    return pl.pallas_call(
        _rmsnorm_body,
        grid=(m // tm,),
        in_specs=[pl.BlockSpec((tm, d), lambda i: (i, 0)), pl.BlockSpec((1, d), lambda i: (0, 0))],
        out_specs=pl.BlockSpec((tm, d), lambda i: (i, 0)),
        out_shape=jax.ShapeDtypeStruct((m, d), BF16),
        compiler_params=_params(("arbitrary",)),
        name="rmsnorm",
    )(x, g.reshape(1, d))


def _col_view(ref, c0, width):
    if len(ref.shape) == 3:
        return ref.at[:, c0 // HEAD_DIM:(c0 + width) // HEAD_DIM, :]
    return ref.at[:, c0:c0 + width]


def _matmul_body(*refs, n_pairs, n_rows, n_cols, n_outs, nk, epilogue):
    xs = refs[0:2 * n_pairs:2]
    ws = refs[1:2 * n_pairs:2]
    p = 2 * n_pairs
    row_refs = refs[p:p + n_rows]
    col_refs = refs[p + n_rows:p + n_rows + n_cols]
    out_refs = refs[p + n_rows + n_cols:p + n_rows + n_cols + n_outs]
    tn = ws[0].shape[1]
    if nk == 1:
        sub = SUB_N if tn % SUB_N == 0 else tn
        for c0 in range(0, tn, sub):
            part = None
            for x_ref, w_ref in zip(xs, ws):
                d = jnp.dot(x_ref[...], w_ref[:, c0:c0 + sub], preferred_element_type=F32)
                part = d if part is None else part + d
            epilogue(part, [_col_view(r, c0, sub) for r in row_refs], [_col_view(r, c0, sub) for r in col_refs],
                     [_col_view(r, c0, sub) for r in out_refs])
        return
    part = None
    for x_ref, w_ref in zip(xs, ws):
        d = jnp.dot(x_ref[...], w_ref[...], preferred_element_type=F32)
        part = d if part is None else part + d
    acc_ref = refs[-1]
    k = pl.program_id(2)

    @pl.when(k == 0)
    def _():
        acc_ref[...] = part

    @pl.when(k > 0)
    def _():
        acc_ref[...] += part

    @pl.when(k == nk - 1)
    def _():
        epilogue(acc_ref[...], row_refs, col_refs, out_refs)


def _matmul(name, pairs, *, n, col_off, tm, tn, nk, epilogue, rows=(), cols=(), outs=()):
    m = pairs[0][0].shape[0]
    assert m % tm == 0 and n % tn == 0 and col_off % tn == 0
    joff = col_off // tn
    in_specs, args = [], []
    for x, w in pairs:
        kdim = x.shape[1]
        assert kdim % nk == 0 and w.shape[0] == kdim
        tk = kdim // nk
        in_specs += [pl.BlockSpec((tm, tk), lambda i, j, k: (i, k)),
                     pl.BlockSpec((tk, tn), lambda i, j, k: (k, j + joff))]
        args += [x, w]
    for r in rows:
        in_specs.append(pl.BlockSpec((tm, tn), lambda i, j, k: (i, j)))
        args.append(r)
    for c in cols:
        in_specs.append(pl.BlockSpec((c.shape[0], tn), lambda i, j, k: (0, j)))
        args.append(c)
    out_specs, out_shape = [], []
    for o in outs:
        if isinstance(o, tuple):
            out_specs.append(pl.BlockSpec((tm, tn // HEAD_DIM, HEAD_DIM), lambda i, j, k: (i, j, 0)))
            out_shape.append(jax.ShapeDtypeStruct((m, n // HEAD_DIM, HEAD_DIM), o[0]))
        else:
            out_specs.append(pl.BlockSpec((tm, tn), lambda i, j, k: (i, j)))
            out_shape.append(jax.ShapeDtypeStruct((m, n), o))
    body = functools.partial(_matmul_body, n_pairs=len(pairs), n_rows=len(rows), n_cols=len(cols),
                             n_outs=len(outs), nk=nk, epilogue=epilogue)
    return pl.pallas_call(
        body,
        grid=(m // tm, n // tn, nk),
        in_specs=in_specs,
        out_specs=out_specs,
        out_shape=out_shape,
        scratch_shapes=[pltpu.VMEM((tm, tn), F32)] if nk > 1 else [],
        compiler_params=_params(("arbitrary", "arbitrary", "arbitrary")),
        name=name,
    )(*args)


def _store_head(o, h, y):
    if len(o.shape) == 3:
        o[:, h, :] = y.astype(o.dtype)
    else:
        o[:, h * HEAD_DIM:(h + 1) * HEAD_DIM] = y.astype(o.dtype)


def _epi_head_norm(acc, rows, cols, outs):
    g = cols[0][...]
    for h in range(acc.shape[1] // HEAD_DIM):
        sl = slice(h * HEAD_DIM, (h + 1) * HEAD_DIM)
        blk = acc[:, sl]
        y = blk * lax.rsqrt(jnp.mean(blk * blk, axis=-1, keepdims=True) + EPS) * g[:, sl]
        for o in outs:
            _store_head(o, h, y)


def _epi_copy(acc, rows, cols, outs):
    for h in range(acc.shape[1] // HEAD_DIM):
        for o in outs:
            _store_head(o, h, acc[:, h * HEAD_DIM:(h + 1) * HEAD_DIM])


def _epi_silu(acc, rows, cols, outs):
    outs[0][...] = (acc * jax.nn.sigmoid(acc)).astype(outs[0].dtype)


def _epi_sigmoid(acc, rows, cols, outs):
    outs[0][...] = jax.nn.sigmoid(acc).astype(outs[0].dtype)


def _epi_forget(acc, rows, cols, outs, *, layer):
    logits = cols[0][...]
    e = jnp.exp(logits - jnp.max(logits, axis=0, keepdims=True))
    lb = jnp.sum(e[:layer + 1], axis=0, keepdims=True) / jnp.sum(e, axis=0, keepdims=True)
    forget = lb + (1.0 - lb) * jax.nn.sigmoid(acc)
    outs[0][...] = jnp.log(forget)
    outs[1][...] = 1.0 - forget


def _epi_residual(acc, rows, cols, outs):
    outs[0][...] = rows[0][...] + acc


def _epi_relu2(acc, rows, cols, outs):
    r = jnp.maximum(acc, 0.0)
    outs[0][...] = (r * r).astype(outs[0].dtype)


def _select_blocks_t(sc, n_valid, topk):
    nb = sc.shape[0]
    row = lax.broadcasted_iota(jnp.int32, sc.shape, 0)
    valid = row < n_valid
    sel = jnp.zeros(sc.shape, F32)
    for j in range(nb):
        sj = sc[j:j + 1, :]
        beats = ((sc > sj) | ((sc == sj) & (row < j))) & valid
        rank = jnp.sum(jnp.where(beats, 1.0, 0.0), axis=0, keepdims=True)
        chosen = jnp.where((rank < float(topk)) & (j < n_valid), 1.0, 0.0)
        sel = jnp.where(row == j, chosen, sel)
    return sel


def _moba_prompt_body(q_ref, k_ref, v_ref, o_ref, kmean_sc, qx_sc, s_sc, mrun_sc, m_sc, l_sc, acc_sc,
                      *, nb, scale):
    p = pl.program_id(2)
    blk = MOBA_BLOCK
    rows = KV_GROUP * blk
    tiles = (p, nb - 1 - p)

    @pl.when(p == 0)
    def _():
        kall = k_ref[...].astype(F32)
        kmean_sc[...] = jnp.sum(kall.reshape(nb, blk, HEAD_DIM), axis=1) * (1.0 / blk)

    lane = lax.broadcasted_iota(jnp.int32, (blk, LANES), 1)

    def keys_ext(j):
        kj = k_ref[pl.ds(pl.multiple_of(j * blk, blk), blk), :]
        return jnp.concatenate([kj, jnp.where(lane == j, 1.0, 0.0).astype(BF16)], axis=1)

    def past_slot(r):
        in_first = r < p
        return jnp.where(in_first, 0, 1), jnp.where(in_first, r, r - p)

    qpos = lax.broadcasted_iota(jnp.int32, (rows, blk), 0) % blk
    kpos = lax.broadcasted_iota(jnp.int32, (rows, blk), 1)
    causal = kpos <= qpos
    kmean = kmean_sc[...].astype(BF16)

    for t, i in enumerate(tiles):
        qb = q_ref[pl.ds(pl.multiple_of(i * blk, blk), blk), :]
        q2 = jnp.concatenate([qb[:, g * HEAD_DIM:(g + 1) * HEAD_DIM] for g in range(KV_GROUP)], axis=0)
        sel_t = _select_blocks_t(_nt_dot(kmean, q2), i, MOBA_TOPK)
        brow = lax.broadcasted_iota(jnp.int32, sel_t.shape, 0)
        bias_t = jnp.where((sel_t > 0.5) | (brow == i), 0.0, NEG)
        pad = jnp.zeros((LANES - nb, rows), F32)
        bias = jnp.concatenate([bias_t, pad], axis=0).T
        qx = jnp.concatenate([q2, bias.astype(BF16)], axis=1)
        qx_sc[t] = qx
        s = jnp.where(causal, _nt_dot(qx, keys_ext(i)), NEG)
        s_sc[t] = s
        mrun_sc[t] = s

    for r in range(nb - 1):
        t, j = past_slot(r)
        s = _nt_dot(qx_sc[t], keys_ext(j))
        s_sc[2 + r] = s
        mrun_sc[t] = jnp.maximum(mrun_sc[t], s)

    for t, i in enumerate(tiles):
        m = jnp.max(mrun_sc[t], axis=-1, keepdims=True)
        m_sc[t] = jnp.broadcast_to(m, (rows, LANES))
        pe = jnp.exp((s_sc[t] - m) * scale)
        l_sc[t] = pe
        acc_sc[t] = jnp.dot(pe.astype(BF16), v_ref[pl.ds(pl.multiple_of(i * blk, blk), blk), :],
                            preferred_element_type=F32)

    for r in range(nb - 1):
        t, j = past_slot(r)
        m = m_sc[t]
        pe = jnp.exp((s_sc[2 + r] - jnp.concatenate([m] * (blk // LANES), axis=1)) * scale)
        l_sc[t] += pe
        acc_sc[t] += jnp.dot(pe.astype(BF16), v_ref[pl.ds(pl.multiple_of(j * blk, blk), blk), :],
                             preferred_element_type=F32)

    for t, i in enumerate(tiles):
        o = acc_sc[t] / jnp.sum(l_sc[t], axis=-1, keepdims=True)
        q0 = pl.multiple_of(i * blk, blk)
        for g in range(KV_GROUP):
            o_ref[pl.ds(q0, blk), g * HEAD_DIM:(g + 1) * HEAD_DIM] = o[g * blk:(g + 1) * blk].astype(o_ref.dtype)


def _moba_prompt(q, kb, vb, *, batch, seq):
    assert seq % (2 * MOBA_BLOCK) == 0, "query tiles are processed in (p, nb-1-p) pairs"
    nb = seq // MOBA_BLOCK
    assert nb <= LANES
    n_kv = kb.shape[1] // HEAD_DIM
    gw = KV_GROUP * HEAD_DIM
    rows = KV_GROUP * MOBA_BLOCK
    body = functools.partial(_moba_prompt_body, nb=nb, scale=HEAD_DIM ** -0.5)
    return pl.pallas_call(
        body,
        grid=(batch, n_kv, nb // 2),
        in_specs=[pl.BlockSpec((seq, gw), lambda b, n, p: (b, n)),
                  pl.BlockSpec((seq, HEAD_DIM), lambda b, n, p: (b, n)),
                  pl.BlockSpec((seq, HEAD_DIM), lambda b, n, p: (b, n))],
        out_specs=pl.BlockSpec((seq, gw), lambda b, n, p: (b, n)),
        out_shape=jax.ShapeDtypeStruct(q.shape, BF16),
        scratch_shapes=[pltpu.VMEM((nb, HEAD_DIM), F32),
                        pltpu.VMEM((2, rows, 2 * HEAD_DIM), BF16),
                        pltpu.VMEM((nb + 1, rows, MOBA_BLOCK), F32),
                        pltpu.VMEM((2, rows, MOBA_BLOCK), F32),
                        pltpu.VMEM((2, rows, LANES), F32),
                        pltpu.VMEM((2, rows, MOBA_BLOCK), F32),
                        pltpu.VMEM((2, rows, HEAD_DIM), F32)],
        compiler_params=_params(("arbitrary", "arbitrary", "arbitrary")),
        name="moba_prompt",
    )(q, kb, vb)


def _moba_sample_body(pt_ref, q_ref, kn_ref, vn_ref, *rest, pps, nch, n_kv, t_new, n_pages, scale):
    k_refs = rest[:pps]
    v_refs = rest[pps:2 * pps]
    o_ref = rest[2 * pps]
    qbd_sc, s_sc, p_sc, km_sc, oacc_sc, l_sc = rest[2 * pps + 1:]
    c = pl.program_id(1)
    ppb = MOBA_BLOCK // PAGE_SIZE
    nb = n_pages // ppb
    gt = KV_GROUP * t_new
    rows = n_kv * gt
    kvw = n_kv * HEAD_DIM

    @pl.when(c == 0)
    def _():
        qv = q_ref[...].astype(F32)
        per_g = [jnp.concatenate([qv[:, (KV_GROUP * n + g) * HEAD_DIM:(KV_GROUP * n + g + 1) * HEAD_DIM]
                                  for n in range(n_kv)], axis=1) for g in range(KV_GROUP)]
        tiled = jnp.concatenate([per_g[g] for n in range(n_kv) for g in range(KV_GROUP)], axis=0)
        rown = lax.broadcasted_iota(jnp.int32, (rows, kvw), 0) // gt
        coln = lax.broadcasted_iota(jnp.int32, (rows, kvw), 1) // HEAD_DIM
        qbd_sc[...] = jnp.where(rown == coln, tiled, 0.0).astype(BF16)

    def page_rows(ref):
        n_tok = ref.shape[0] // n_kv
        return jnp.concatenate([ref[pl.ds(n, n_tok, stride=n_kv), :] for n in range(n_kv)], axis=1)

    @pl.when(c < nch)
    def _():
        qbd = qbd_sc[...]
        colsum = None
        for i in range(pps):
            cs = jnp.sum(k_refs[i][...].reshape(PAGE_SIZE, n_kv, HEAD_DIM), axis=0)
            colsum = cs if i % ppb == 0 else colsum + cs
            if i % ppb == ppb - 1:
                km_sc[pl.ds(c * (pps // ppb) + i // ppb, 1)] = colsum[None]
            col0 = pl.multiple_of((c * pps + i) * PAGE_SIZE, PAGE_SIZE)
            s_sc[:, pl.ds(col0, PAGE_SIZE)] = _nt_dot(qbd, page_rows(k_refs[i]).astype(BF16))

    @pl.when(c == nch - 1)
    def _():
        qbd = qbd_sc[...]
        kmean = (jnp.concatenate([km_sc[:, n, :] for n in range(n_kv)], axis=1)
                 * (1.0 / MOBA_BLOCK)).astype(BF16)
        sc_t = _nt_dot(kmean, qbd)
        sel_t = _select_blocks_t(sc_t, nb, MOBA_TOPK)
        bias = jnp.where(sel_t > 0.5, 0.0, NEG).T
        zpad = jnp.zeros((LANES - t_new, kvw), F32)
        knp = jnp.concatenate([kn_ref[...].astype(F32), zpad], axis=0).astype(BF16)
        vnp = jnp.concatenate([vn_ref[...].astype(F32), zpad], axis=0).astype(BF16)
        s_own = _nt_dot(qbd, knp)
        tok = lax.broadcasted_iota(jnp.int32, s_own.shape, 0) % t_new
        key = lax.broadcasted_iota(jnp.int32, s_own.shape, 1)
        s_own = jnp.where(key <= tok, s_own, NEG)
        m = jnp.max(s_own, axis=-1, keepdims=True)
        for j in range(nb):
            sl = slice(j * MOBA_BLOCK, (j + 1) * MOBA_BLOCK)
            sb = s_sc[:, sl] + bias[:, j:j + 1]
            s_sc[:, sl] = sb
            m = jnp.maximum(m, jnp.max(sb, axis=-1, keepdims=True))
        p_own = jnp.exp((s_own - m) * scale)
        l = jnp.sum(p_own, axis=-1, keepdims=True)
        for j in range(nb):
            sl = slice(j * MOBA_BLOCK, (j + 1) * MOBA_BLOCK)
            p = jnp.exp((s_sc[:, sl] - m) * scale)
            l = l + jnp.sum(p, axis=-1, keepdims=True)
            p_sc[:, sl] = p.astype(BF16)
        l_sc[...] = jnp.broadcast_to(l, l_sc.shape)
        oacc_sc[...] = jnp.dot(p_own.astype(BF16), vnp, preferred_element_type=F32)

    @pl.when(c >= nch)
    def _():
        acc = oacc_sc[...]
        for i in range(pps):
            col0 = pl.multiple_of(((c - nch) * pps + i) * PAGE_SIZE, PAGE_SIZE)
            acc = acc + jnp.dot(p_sc[:, pl.ds(col0, PAGE_SIZE)], page_rows(v_refs[i]).astype(BF16),
                                preferred_element_type=F32)
        oacc_sc[...] = acc

    @pl.when(c == 2 * nch - 1)
    def _():
        o = oacc_sc[...] / l_sc[:, :1]
        for n in range(n_kv):
            for g in range(KV_GROUP):
                r0 = n * gt + g * t_new
                h = KV_GROUP * n + g
                o_ref[:, h * HEAD_DIM:(h + 1) * HEAD_DIM] = (
                    o[r0:r0 + t_new, n * HEAD_DIM:(n + 1) * HEAD_DIM].astype(o_ref.dtype))


def _moba_sample(q, kn, vn, cache_k, cache_v, page_table, *, batch, t_new):
    n_pages = page_table.shape[1]
    ppb = MOBA_BLOCK // PAGE_SIZE
    assert n_pages % ppb == 0 and n_pages // ppb >= MOBA_TOPK
    kvw = kn.shape[1]
    n_kv = kvw // HEAD_DIM
    pps = _tile(n_pages, 8, ppb)
    nch = n_pages // pps
    rows = n_kv * KV_GROUP * t_new
    past = n_pages * PAGE_SIZE

    def k_map(i):
        return lambda b, c, pt: (pt[b, jnp.minimum(c, nch - 1) * pps + i], 0)

    def v_map(i):
        return lambda b, c, pt: (pt[b, jnp.maximum(c - nch, 0) * pps + i], 0)

    in_specs = [pl.BlockSpec((t_new, q.shape[1]), lambda b, c, pt: (b, 0)),
                pl.BlockSpec((t_new, kvw), lambda b, c, pt: (b, 0)),
                pl.BlockSpec((t_new, kvw), lambda b, c, pt: (b, 0))]
    in_specs += [pl.BlockSpec((PAGE_SIZE * n_kv, HEAD_DIM), k_map(i)) for i in range(pps)]
    in_specs += [pl.BlockSpec((PAGE_SIZE * n_kv, HEAD_DIM), v_map(i)) for i in range(pps)]
    body = functools.partial(_moba_sample_body, pps=pps, nch=nch, n_kv=n_kv, t_new=t_new,
                             n_pages=n_pages, scale=HEAD_DIM ** -0.5)
    grid_spec = pltpu.PrefetchScalarGridSpec(
        num_scalar_prefetch=1,
        grid=(batch, 2 * nch),
        in_specs=in_specs,
        out_specs=pl.BlockSpec((t_new, q.shape[1]), lambda b, c, pt: (b, 0)),
        scratch_shapes=[pltpu.VMEM((rows, kvw), BF16),
                        pltpu.VMEM((rows, past), F32),
                        pltpu.VMEM((rows, past), BF16),
                        pltpu.VMEM((n_pages // ppb, n_kv, HEAD_DIM), F32),
                        pltpu.VMEM((rows, kvw), F32),
                        pltpu.VMEM((rows, LANES), F32)],
    )
    return pl.pallas_call(
        body,
        grid_spec=grid_spec,
        out_shape=jax.ShapeDtypeStruct(q.shape, BF16),
        compiler_params=_params(("arbitrary", "arbitrary")),
        name="moba_sample",
    )(page_table, q, kn, vn, *([cache_k] * pps), *([cache_v] * pps))


def _hgrn_body(q_ref, lf_ref, k_ref, v_ref, gate_ref, gn_ref, h0_ref, o_ref, hout_ref, ht_sc,
               *, hb, chunk, sub, n_chunks, n_tsteps):
    t = pl.program_id(2)
    ns = chunk // sub

    @pl.when(t == 0)
    def _():
        for h in range(hb):
            ht_sc[h] = h0_ref[h].T

    r_i = lax.broadcasted_iota(jnp.int32, (chunk, chunk), 0)
    c_i = lax.broadcasted_iota(jnp.int32, (chunk, chunk), 1)
    causal = c_i <= r_i
    tri = jnp.where(causal, 1.0, 0.0).astype(BF16)
    rowc = lax.broadcasted_iota(jnp.int32, (chunk, HEAD_DIM), 0)

    def one_chunk(ci, carry):
        r0 = pl.multiple_of(ci * chunk, chunk)
        for h in range(hb):
            cs = slice(h * HEAD_DIM, (h + 1) * HEAD_DIM)
            lf = lf_ref[pl.ds(r0, chunk), cs]
            q = q_ref[pl.ds(r0, chunk), cs]
            k = k_ref[pl.ds(r0, chunk), cs]
            v = v_ref[pl.ds(r0, chunk), cs]
            hi = lf.astype(BF16)
            lo = (lf - hi.astype(F32)).astype(BF16)
            bb = jnp.dot(tri, jnp.concatenate([hi, lo], axis=1), preferred_element_type=F32)
            b = bb[:, :HEAD_DIM] + bb[:, HEAD_DIM:]
            b_last = b[chunk - 1:chunk, :]
            ht = ht_sc[h]
            qe = (q * jnp.exp(b)).astype(BF16)
            o = _nt_dot(qe, ht.astype(BF16))
            refs = [jnp.zeros((1, HEAD_DIM), F32)] + [b[sub * s - 1:sub * s, :] for s in range(1, ns)]
            gfull = jnp.concatenate([jnp.broadcast_to(r, (sub, HEAD_DIM)) for r in refs], axis=0)
            qh = q * jnp.exp(b - gfull)
            qcat = jnp.concatenate(
                [jnp.where((rowc >= sub * s) & (rowc < sub * (s + 1)), qh, 0.0) for s in range(ns)],
                axis=1).astype(BF16)
            kcat = jnp.concatenate(
                [jnp.where(rowc < sub * (s + 1), k * jnp.exp(refs[s] - b), 0.0) for s in range(ns)],
                axis=1).astype(BF16)
            att = jnp.where(causal, _nt_dot(qcat, kcat), 0.0)
            o = o + jnp.dot(att.astype(BF16), v, preferred_element_type=F32)
            kd = (k * jnp.exp(b_last - b)).astype(BF16)
            ht_sc[h] = ht * jnp.exp(b_last) + _tn_dot(v, kd)
            y = o * lax.rsqrt(jnp.mean(o * o, axis=-1, keepdims=True) + EPS) * gn_ref[:, cs]
            o_ref[pl.ds(r0, chunk), cs] = (y * gate_ref[pl.ds(r0, chunk), cs]).astype(o_ref.dtype)
        return carry

    lax.fori_loop(0, n_chunks, one_chunk, 0)

    @pl.when(t == n_tsteps - 1)
    def _():
        for h in range(hb):
            hout_ref[h] = ht_sc[h].T


def _hgrn(hq, lf, hk, hv, gate, gnorm, h0, *, batch, seq):
    width = hq.shape[1]
    n_heads = width // HEAD_DIM
    hb = _tile(n_heads, 16)
    chunk = HG_CHUNK if seq % HG_CHUNK == 0 else seq
    sub = HG_SUB if chunk % HG_SUB == 0 else chunk
    tc = _tile(seq, 256, chunk)
    nt = seq // tc
    bw = hb * HEAD_DIM
    row_spec = pl.BlockSpec((tc, bw), lambda b, g, t: (b * nt + t, g))
    st_spec = pl.BlockSpec((None, hb, HEAD_DIM, HEAD_DIM), lambda b, g, t: (b, g, 0, 0))
    body = functools.partial(_hgrn_body, hb=hb, chunk=chunk, sub=sub, n_chunks=tc // chunk, n_tsteps=nt)
    return pl.pallas_call(
        body,
        grid=(batch, n_heads // hb, nt),
        in_specs=[row_spec, row_spec, row_spec, row_spec, row_spec,
                  pl.BlockSpec((1, bw), lambda b, g, t: (0, g)), st_spec],
        out_specs=[row_spec, st_spec],
        out_shape=[jax.ShapeDtypeStruct(hq.shape, BF16), jax.ShapeDtypeStruct(h0.shape, F32)],
        scratch_shapes=[pltpu.VMEM((hb, HEAD_DIM, HEAD_DIM), F32)],
        compiler_params=_params(("arbitrary", "arbitrary", "arbitrary")),
        name="hgrn",
    )(hq, lf, hk, hv, gate, gnorm, h0)


def _group_inputs(x, w, *, tm):
    d = x.shape[1]
    attn_w = d // 2
    kv_w = attn_w // KV_GROUP
    hg_w = d - attn_w
    xn = _rmsnorm(x, w["norm1_g"], _tile(x.shape[0], 256, 8))
    starts = [0, attn_w, attn_w + kv_w, attn_w + 2 * kv_w, attn_w + 2 * kv_w + hg_w,
              attn_w + 2 * kv_w + 2 * hg_w, attn_w + 2 * kv_w + 3 * hg_w]

    def proj(name, seg, n, epilogue, cols=(), outs=(), whole=False):
        tn = n if whole else _tile(n, 1024 if len(outs) == 1 else 512, LANES)
        while starts[seg] % tn:
            tn = _tile(n, tn - 1, LANES)
        assert tn == n or not whole
        tm_call = _tile(xn.shape[0], min(tm, 512), 8) if whole else tm
        return _matmul(name, [(xn, w["w_in"])], n=n, col_off=starts[seg], tm=tm_call, tn=tn, nk=1,
                       epilogue=epilogue, cols=cols, outs=outs)

    (q,) = proj("proj_q", 0, attn_w, _epi_head_norm, cols=(w["q_gain"],), outs=(BF16,))
    k, kb = proj("proj_k", 1, kv_w, _epi_head_norm, cols=(w["k_gain"],), outs=((F32, "heads"), BF16),
                 whole=True)
    v, vb = proj("proj_v", 2, kv_w, _epi_copy, outs=((F32, "heads"), BF16), whole=True)
    (hq,) = proj("proj_hq", 3, hg_w, _epi_silu, outs=(F32,))
    lf, hk = proj("proj_hf", 4, hg_w, functools.partial(_epi_forget, layer=0),
                  cols=(w["lb_logits"],), outs=(F32, F32))
    (hv,) = proj("proj_hi", 5, hg_w, _epi_copy, outs=(BF16,))
    (gate,) = proj("proj_hg", 6, hg_w, _epi_sigmoid, outs=(F32,))
    return dict(q=q, k=k, kb=kb, v=v, vb=vb, hq=hq, lf=lf, hk=hk, hv=hv, gate=gate)


def _group_outputs(x, attn_o, hg_o, w, *, tm):
    m, d = x.shape
    d_ff = w["w_up"].shape[1]
    tn = _tile(d, 512, LANES)
    (x1,) = _matmul("proj_out", [(attn_o, w["w_out_a"]), (hg_o, w["w_out_h"])], n=d, col_off=0, tm=tm,
                    tn=tn, nk=1, epilogue=_epi_residual, rows=(x,), outs=(F32,))
    hn = _rmsnorm(x1, w["norm2_g"], _tile(m, 256, 8))
    (act,) = _matmul("mlp_up", [(hn, w["w_up"])], n=d_ff, col_off=0, tm=tm, tn=_tile(d_ff, 1024, LANES),
                     nk=1, epilogue=_epi_relu2, outs=(BF16,))
    nk = d_ff // _tile(d_ff, 2048, LANES)
    (y,) = _matmul("mlp_down", [(act, w["w_down"])], n=d, col_off=0, tm=tm, tn=_tile(d, 1024, LANES),
                   nk=nk, epilogue=_epi_residual, rows=(x1,), outs=(F32,))
    return y


def kernel(x_prompt, x_sample, cache_k, cache_v, state_h, page_table, norm1_g, w_in, q_norm_g, k_norm_g,
           lb_logits, hg_norm_g, w_out, norm2_g, w_up, w_down):
    depth = w_in.shape[0]
    assert depth == 1, "one layer"
    bp, tp, d = x_prompt.shape
    bs, ts, _ = x_sample.shape
    attn_w = d // 2
    n_q = attn_w // HEAD_DIM
    n_kv = n_q // KV_GROUP
    kv_w = n_kv * HEAD_DIM
    hg_w = d - attn_w
    n_hg = hg_w // HEAD_DIM

    w = dict(
        norm1_g=norm1_g[0], norm2_g=norm2_g[0],
        w_in=w_in[0].astype(BF16),
        w_out_a=w_out[0, :attn_w].astype(BF16), w_out_h=w_out[0, attn_w:].astype(BF16),
        w_up=w_up[0].astype(BF16), w_down=w_down[0].astype(BF16),
        q_gain=jnp.tile(q_norm_g[0], n_q).reshape(1, attn_w),
        k_gain=jnp.tile(k_norm_g[0], n_kv).reshape(1, kv_w),
        lb_logits=lb_logits,
    )
    gnorm = hg_norm_g[0].reshape(1, hg_w)

    xp = x_prompt.reshape(bp * tp, d)
    tm_p = _tile(bp * tp, 1024, 8)
    gp = _group_inputs(xp, w, tm=tm_p)
    attn_p = _moba_prompt(gp["q"], gp["kb"], gp["vb"], batch=bp, seq=tp)
    h0_p = jnp.zeros((bp, n_hg, HEAD_DIM, HEAD_DIM), F32)
    hg_p, h_p = _hgrn(gp["hq"], gp["lf"], gp["hk"], gp["hv"], gp["gate"], gnorm, h0_p, batch=bp, seq=tp)
    y_p = _group_outputs(xp, attn_p, hg_p, w, tm=tm_p)

    xs = x_sample.reshape(bs * ts, d)
    tm_s = _tile(bs * ts, 256, 8)
    gs = _group_inputs(xs, w, tm=tm_s)
    attn_s = _moba_sample(gs["q"], gs["kb"], gs["vb"], cache_k[0].reshape(-1, HEAD_DIM),
                          cache_v[0].reshape(-1, HEAD_DIM), page_table, batch=bs, t_new=ts)
    hg_s, h_s = _hgrn(gs["hq"], gs["lf"], gs["hk"], gs["hv"], gs["gate"], gnorm, state_h[0], batch=bs, seq=ts)
    y_s = _group_outputs(xs, attn_s, hg_s, w, tm=tm_s)

    return (y_p.reshape(bp, tp, d), y_s.reshape(bs, ts, d),
            gp["k"].reshape(1, bp, tp, n_kv, HEAD_DIM), gp["v"].reshape(1, bp, tp, n_kv, HEAD_DIM),
            h_p.reshape(1, bp, n_hg, HEAD_DIM, HEAD_DIM),
            gs["k"].reshape(1, bs, ts, n_kv, HEAD_DIM), gs["v"].reshape(1, bs, ts, n_kv, HEAD_DIM),
            h_s.reshape(1, bs, n_hg, HEAD_DIM, HEAD_DIM))
```

```python
import functools

import jax
import jax.numpy as jnp
from jax import lax
from jax.experimental import pallas as pl
from jax.experimental.pallas import tpu as pltpu

F32 = jnp.float32
BF16 = jnp.bfloat16

HEAD_DIM = 128
KV_GROUP = 2
PAGE_SIZE = 128
MOBA_BLOCK = 256
MOBA_TOPK = 3
HG_CHUNK = 64
HG_SUB = 16
EPS = 1e-6
NEG = -1e30
LANES = 128
SUB_N = 512
VMEM_LIMIT = 56 * 1024 * 1024


def _tile(n, target, mult=1):
    for t in range(min(n, target), 0, -1):
        if n % t == 0 and t % mult == 0:
            return t
    return n


def _params(sem):
    return pltpu.CompilerParams(dimension_semantics=sem, vmem_limit_bytes=VMEM_LIMIT)


def _nt_dot(a, b):
    return lax.dot_general(a, b, (((1,), (1,)), ((), ())), preferred_element_type=F32)


def _tn_dot(a, b):
    return lax.dot_general(a, b, (((0,), (0,)), ((), ())), preferred_element_type=F32)


def _rmsnorm_body(x_ref, g_ref, o_ref):
    x = x_ref[...]
    ms = jnp.mean(x * x, axis=-1, keepdims=True)
    o_ref[...] = (x * lax.rsqrt(ms + EPS) * g_ref[...]).astype(o_ref.dtype)


def _rmsnorm(x, g, tm):
    m, d = x.shape
    return pl.pallas_call(
        _rmsnorm_body,
        grid=(m // tm,),
        in_specs=[pl.BlockSpec((tm, d), lambda i: (i, 0)), pl.BlockSpec((1, d), lambda i: (0, 0))],
        out_specs=pl.BlockSpec((tm, d), lambda i: (i, 0)),
        out_shape=jax.ShapeDtypeStruct((m, d), BF16),
        compiler_params=_params(("arbitrary",)),
        name="rmsnorm",
    )(x, g.reshape(1, d))


def _col_view(ref, c0, width):
    if len(ref.shape) == 3:
        return ref.at[:, c0 // HEAD_DIM:(c0 + width) // HEAD_DIM, :]
    return ref.at[:, c0:c0 + width]


def _matmul_body(*refs, n_pairs, n_rows, n_cols, n_outs, nk, epilogue):
    xs = refs[0:2 * n_pairs:2]
    ws = refs[1:2 * n_pairs:2]
    p = 2 * n_pairs
    row_refs = refs[p:p + n_rows]
    col_refs = refs[p + n_rows:p + n_rows + n_cols]
    out_refs = refs[p + n_rows + n_cols:p + n_rows + n_cols + n_outs]
    tn = ws[0].shape[1]
    if nk == 1:
        sub = SUB_N if tn % SUB_N == 0 else tn
        for c0 in range(0, tn, sub):
            part = None
            for x_ref, w_ref in zip(xs, ws):
                d = jnp.dot(x_ref[...], w_ref[:, c0:c0 + sub].astype(BF16), preferred_element_type=F32)
                part = d if part is None else part + d
            epilogue(part, [_col_view(r, c0, sub) for r in row_refs], [_col_view(r, c0, sub) for r in col_refs],
                     [_col_view(r, c0, sub) for r in out_refs])
        return
    part = None
    for x_ref, w_ref in zip(xs, ws):
        d = jnp.dot(x_ref[...], w_ref[...].astype(BF16), preferred_element_type=F32)
        part = d if part is None else part + d
    acc_ref = refs[-1]
    k = pl.program_id(2)

    @pl.when(k == 0)
    def _():
        acc_ref[...] = part

    @pl.when(k > 0)
    def _():
        acc_ref[...] += part

    @pl.when(k == nk - 1)
    def _():
        epilogue(acc_ref[...], row_refs, col_refs, out_refs)


def _matmul(name, pairs, *, n, col_off, tm, tn, nk, epilogue, rows=(), cols=(), outs=()):
    m = pairs[0][0].shape[0]
    assert m % tm == 0 and n % tn == 0 and col_off % tn == 0
    joff = col_off // tn
    in_specs, args = [], []
    for x, w, row_off in pairs:
        kdim = x.shape[1]
        assert kdim % nk == 0
        tk = kdim // nk
        assert row_off % tk == 0
        in_specs += [pl.BlockSpec((tm, tk), lambda i, j, k: (i, k)),
                     pl.BlockSpec((tk, tn), lambda i, j, k, koff=row_off // tk: (k + koff, j + joff))]
        args += [x, w]
    for r in rows:
        in_specs.append(pl.BlockSpec((tm, tn), lambda i, j, k: (i, j)))
        args.append(r)
    for c in cols:
        in_specs.append(pl.BlockSpec((c.shape[0], tn), lambda i, j, k: (0, j)))
        args.append(c)
    out_specs, out_shape = [], []
    for o in outs:
        if isinstance(o, tuple):
            out_specs.append(pl.BlockSpec((tm, tn // HEAD_DIM, HEAD_DIM), lambda i, j, k: (i, j, 0)))
            out_shape.append(jax.ShapeDtypeStruct((m, n // HEAD_DIM, HEAD_DIM), o[0]))
        else:
            out_specs.append(pl.BlockSpec((tm, tn), lambda i, j, k: (i, j)))
            out_shape.append(jax.ShapeDtypeStruct((m, n), o))
    body = functools.partial(_matmul_body, n_pairs=len(pairs), n_rows=len(rows), n_cols=len(cols),
                             n_outs=len(outs), nk=nk, epilogue=epilogue)
    return pl.pallas_call(
        body,
        grid=(m // tm, n // tn, nk),
        in_specs=in_specs,
        out_specs=out_specs,
        out_shape=out_shape,
        scratch_shapes=[pltpu.VMEM((tm, tn), F32)] if nk > 1 else [],
        compiler_params=_params(("arbitrary", "arbitrary", "arbitrary")),
        name=name,
    )(*args)


def _store_head(o, h, y):
    if len(o.shape) == 3:
        o[:, h, :] = y.astype(o.dtype)
    else:
        o[:, h * HEAD_DIM:(h + 1) * HEAD_DIM] = y.astype(o.dtype)


def _epi_head_norm(acc, rows, cols, outs):
    g = cols[0][...]
    for h in range(acc.shape[1] // HEAD_DIM):
        sl = slice(h * HEAD_DIM, (h + 1) * HEAD_DIM)
        blk = acc[:, sl]
        y = blk * lax.rsqrt(jnp.mean(blk * blk, axis=-1, keepdims=True) + EPS) * g[:, sl]
        for o in outs:
            _store_head(o, h, y)


def _epi_copy(acc, rows, cols, outs):
    for h in range(acc.shape[1] // HEAD_DIM):
        for o in outs:
            _store_head(o, h, acc[:, h * HEAD_DIM:(h + 1) * HEAD_DIM])


def _epi_silu(acc, rows, cols, outs):
    outs[0][...] = (acc * jax.nn.sigmoid(acc)).astype(outs[0].dtype)


def _epi_sigmoid(acc, rows, cols, outs):
    outs[0][...] = jax.nn.sigmoid(acc).astype(outs[0].dtype)


def _epi_forget(acc, rows, cols, outs, *, layer):
    logits = cols[0][...]
    e = jnp.exp(logits - jnp.max(logits, axis=0, keepdims=True))
    lb = jnp.sum(e[:layer + 1], axis=0, keepdims=True) / jnp.sum(e, axis=0, keepdims=True)
    forget = lb + (1.0 - lb) * jax.nn.sigmoid(acc)
    outs[0][...] = jnp.log(forget)
    outs[1][...] = 1.0 - forget


def _epi_residual(acc, rows, cols, outs):
    outs[0][...] = rows[0][...] + acc


def _epi_relu2(acc, rows, cols, outs):
    r = jnp.maximum(acc, 0.0)
    outs[0][...] = (r * r).astype(outs[0].dtype)


def _select_blocks_t(sc, n_valid, topk):
    nb = sc.shape[0]
    row = lax.broadcasted_iota(jnp.int32, sc.shape, 0)
    valid = row < n_valid
    sel = jnp.zeros(sc.shape, F32)
    for j in range(nb):
        sj = sc[j:j + 1, :]
        beats = ((sc > sj) | ((sc == sj) & (row < j))) & valid
        rank = jnp.sum(jnp.where(beats, 1.0, 0.0), axis=0, keepdims=True)
        chosen = jnp.where((rank < float(topk)) & (j < n_valid), 1.0, 0.0)
        sel = jnp.where(row == j, chosen, sel)
    return sel


def _moba_prompt_body(q_ref, k_ref, v_ref, o_ref, kmean_sc, qx_sc, s_sc, mrun_sc, m_sc, l_sc, acc_sc,
                      *, nb, scale):
    p = pl.program_id(2)
    blk = MOBA_BLOCK
    rows = KV_GROUP * blk
    tiles = (p, nb - 1 - p)

    @pl.when(p == 0)
    def _():
        kall = k_ref[...].astype(F32)
        kmean_sc[...] = jnp.sum(kall.reshape(nb, blk, HEAD_DIM), axis=1) * (1.0 / blk)

    lane = lax.broadcasted_iota(jnp.int32, (blk, LANES), 1)

    def keys_ext(j):
        kj = k_ref[pl.ds(pl.multiple_of(j * blk, blk), blk), :]
        return jnp.concatenate([kj, jnp.where(lane == j, 1.0, 0.0).astype(BF16)], axis=1)

    def past_slot(r):
        in_first = r < p
        return jnp.where(in_first, 0, 1), jnp.where(in_first, r, r - p)

    qpos = lax.broadcasted_iota(jnp.int32, (rows, blk), 0) % blk
    kpos = lax.broadcasted_iota(jnp.int32, (rows, blk), 1)
    causal = kpos <= qpos
    kmean = kmean_sc[...].astype(BF16)

    for t, i in enumerate(tiles):
        qb = q_ref[pl.ds(pl.multiple_of(i * blk, blk), blk), :]
        q2 = jnp.concatenate([qb[:, g * HEAD_DIM:(g + 1) * HEAD_DIM] for g in range(KV_GROUP)], axis=0)
        sel_t = _select_blocks_t(_nt_dot(kmean, q2), i, MOBA_TOPK)
        brow = lax.broadcasted_iota(jnp.int32, sel_t.shape, 0)
        bias_t = jnp.where((sel_t > 0.5) | (brow == i), 0.0, NEG)
        pad = jnp.zeros((LANES - nb, rows), F32)
        bias = jnp.concatenate([bias_t, pad], axis=0).T
        qx = jnp.concatenate([q2, bias.astype(BF16)], axis=1)
        qx_sc[t] = qx
        s = jnp.where(causal, _nt_dot(qx, keys_ext(i)), NEG)
        s_sc[t] = s
        mrun_sc[t] = s

    for r in range(nb - 1):
        t, j = past_slot(r)
        s = _nt_dot(qx_sc[t], keys_ext(j))
        s_sc[2 + r] = s
        mrun_sc[t] = jnp.maximum(mrun_sc[t], s)

    for t, i in enumerate(tiles):
        m = jnp.max(mrun_sc[t], axis=-1, keepdims=True)
        m_sc[t] = jnp.broadcast_to(m, (rows, LANES))
        pe = jnp.exp((s_sc[t] - m) * scale)
        l_sc[t] = pe
        acc_sc[t] = jnp.dot(pe.astype(BF16), v_ref[pl.ds(pl.multiple_of(i * blk, blk), blk), :],
                            preferred_element_type=F32)

    for r in range(nb - 1):
        t, j = past_slot(r)
        m = m_sc[t]
        pe = jnp.exp((s_sc[2 + r] - jnp.concatenate([m] * (blk // LANES), axis=1)) * scale)
        l_sc[t] += pe
        acc_sc[t] += jnp.dot(pe.astype(BF16), v_ref[pl.ds(pl.multiple_of(j * blk, blk), blk), :],
                             preferred_element_type=F32)

    for t, i in enumerate(tiles):
        o = acc_sc[t] / jnp.sum(l_sc[t], axis=-1, keepdims=True)
        q0 = pl.multiple_of(i * blk, blk)
        for g in range(KV_GROUP):
            o_ref[pl.ds(q0, blk), g * HEAD_DIM:(g + 1) * HEAD_DIM] = o[g * blk:(g + 1) * blk].astype(o_ref.dtype)


def _moba_prompt(q, kb, vb, *, batch, seq):
    assert seq % (2 * MOBA_BLOCK) == 0, "query tiles are processed in (p, nb-1-p) pairs"
    nb = seq // MOBA_BLOCK
    assert nb <= LANES
    n_kv = kb.shape[1] // HEAD_DIM
    gw = KV_GROUP * HEAD_DIM
    rows = KV_GROUP * MOBA_BLOCK
    body = functools.partial(_moba_prompt_body, nb=nb, scale=HEAD_DIM ** -0.5)
    return pl.pallas_call(
        body,
        grid=(batch, n_kv, nb // 2),
        in_specs=[pl.BlockSpec((seq, gw), lambda b, n, p: (b, n)),
                  pl.BlockSpec((seq, HEAD_DIM), lambda b, n, p: (b, n)),
                  pl.BlockSpec((seq, HEAD_DIM), lambda b, n, p: (b, n))],
        out_specs=pl.BlockSpec((seq, gw), lambda b, n, p: (b, n)),
        out_shape=jax.ShapeDtypeStruct(q.shape, BF16),
        scratch_shapes=[pltpu.VMEM((nb, HEAD_DIM), F32),
                        pltpu.VMEM((2, rows, 2 * HEAD_DIM), BF16),
                        pltpu.VMEM((nb + 1, rows, MOBA_BLOCK), F32),
                        pltpu.VMEM((2, rows, MOBA_BLOCK), F32),
                        pltpu.VMEM((2, rows, LANES), F32),
                        pltpu.VMEM((2, rows, MOBA_BLOCK), F32),
                        pltpu.VMEM((2, rows, HEAD_DIM), F32)],
        compiler_params=_params(("arbitrary", "arbitrary", "arbitrary")),
        name="moba_prompt",
    )(q, kb, vb)


def _moba_sample_body(pt_ref, q_ref, kn_ref, vn_ref, *rest, pps, nch, n_kv, t_new, n_pages, scale):
    k_refs = rest[:pps]
    v_refs = rest[pps:2 * pps]
    o_ref = rest[2 * pps]
    qbd_sc, s_sc, p_sc, km_sc, oacc_sc, l_sc = rest[2 * pps + 1:]
    c = pl.program_id(1)
    ppb = MOBA_BLOCK // PAGE_SIZE
    nb = n_pages // ppb
    gt = KV_GROUP * t_new
    rows = n_kv * gt
    kvw = n_kv * HEAD_DIM

    @pl.when(c == 0)
    def _():
        qv = q_ref[...].astype(F32)
        per_g = [jnp.concatenate([qv[:, (KV_GROUP * n + g) * HEAD_DIM:(KV_GROUP * n + g + 1) * HEAD_DIM]
                                  for n in range(n_kv)], axis=1) for g in range(KV_GROUP)]
        tiled = jnp.concatenate([per_g[g] for n in range(n_kv) for g in range(KV_GROUP)], axis=0)
        rown = lax.broadcasted_iota(jnp.int32, (rows, kvw), 0) // gt
        coln = lax.broadcasted_iota(jnp.int32, (rows, kvw), 1) // HEAD_DIM
        qbd_sc[...] = jnp.where(rown == coln, tiled, 0.0).astype(BF16)

    def page_rows(ref):
        n_tok = ref.shape[0] // n_kv
        return jnp.concatenate([ref[pl.ds(n, n_tok, stride=n_kv), :] for n in range(n_kv)], axis=1)

    @pl.when(c < nch)
    def _():
        qbd = qbd_sc[...]
        colsum = None
        for i in range(pps):
            cs = jnp.sum(k_refs[i][...].reshape(PAGE_SIZE, n_kv, HEAD_DIM), axis=0)
            colsum = cs if i % ppb == 0 else colsum + cs
            if i % ppb == ppb - 1:
                km_sc[pl.ds(c * (pps // ppb) + i // ppb, 1)] = colsum[None]
            col0 = pl.multiple_of((c * pps + i) * PAGE_SIZE, PAGE_SIZE)
            s_sc[:, pl.ds(col0, PAGE_SIZE)] = _nt_dot(qbd, page_rows(k_refs[i]).astype(BF16))

    @pl.when(c == nch - 1)
    def _():
        qbd = qbd_sc[...]
        kmean = (jnp.concatenate([km_sc[:, n, :] for n in range(n_kv)], axis=1)
                 * (1.0 / MOBA_BLOCK)).astype(BF16)
        sc_t = _nt_dot(kmean, qbd)
        sel_t = _select_blocks_t(sc_t, nb, MOBA_TOPK)
        bias = jnp.where(sel_t > 0.5, 0.0, NEG).T
        zpad = jnp.zeros((LANES - t_new, kvw), F32)
        knp = jnp.concatenate([kn_ref[...].astype(F32), zpad], axis=0).astype(BF16)
        vnp = jnp.concatenate([vn_ref[...].astype(F32), zpad], axis=0).astype(BF16)
        s_own = _nt_dot(qbd, knp)
        tok = lax.broadcasted_iota(jnp.int32, s_own.shape, 0) % t_new
        key = lax.broadcasted_iota(jnp.int32, s_own.shape, 1)
        s_own = jnp.where(key <= tok, s_own, NEG)
        m = jnp.max(s_own, axis=-1, keepdims=True)
        for j in range(nb):
            sl = slice(j * MOBA_BLOCK, (j + 1) * MOBA_BLOCK)
            sb = s_sc[:, sl] + bias[:, j:j + 1]
            s_sc[:, sl] = sb
            m = jnp.maximum(m, jnp.max(sb, axis=-1, keepdims=True))
        p_own = jnp.exp((s_own - m) * scale)
        l = jnp.sum(p_own, axis=-1, keepdims=True)
        for j in range(nb):
            sl = slice(j * MOBA_BLOCK, (j + 1) * MOBA_BLOCK)
            p = jnp.exp((s_sc[:, sl] - m) * scale)
            l = l + jnp.sum(p, axis=-1, keepdims=True)
            p_sc[:, sl] = p.astype(BF16)
        l_sc[...] = jnp.broadcast_to(l, l_sc.shape)
        oacc_sc[...] = jnp.dot(p_own.astype(BF16), vnp, preferred_element_type=F32)

    @pl.when(c >= nch)
    def _():
        acc = oacc_sc[...]
        for i in range(pps):
            col0 = pl.multiple_of(((c - nch) * pps + i) * PAGE_SIZE, PAGE_SIZE)
            acc = acc + jnp.dot(p_sc[:, pl.ds(col0, PAGE_SIZE)], page_rows(v_refs[i]).astype(BF16),
                                preferred_element_type=F32)
        oacc_sc[...] = acc

    @pl.when(c == 2 * nch - 1)
    def _():
        o = oacc_sc[...] / l_sc[:, :1]
        for n in range(n_kv):
            for g in range(KV_GROUP):
                r0 = n * gt + g * t_new
                h = KV_GROUP * n + g
                o_ref[:, h * HEAD_DIM:(h + 1) * HEAD_DIM] = (
                    o[r0:r0 + t_new, n * HEAD_DIM:(n + 1) * HEAD_DIM].astype(o_ref.dtype))


def _moba_sample(q, kn, vn, cache_k, cache_v, page_table, *, batch, t_new):
    n_pages = page_table.shape[1]
    ppb = MOBA_BLOCK // PAGE_SIZE
    assert n_pages % ppb == 0 and n_pages // ppb >= MOBA_TOPK
    kvw = kn.shape[1]
    n_kv = kvw // HEAD_DIM
    pps = _tile(n_pages, 16, ppb)
    nch = n_pages // pps
    rows = n_kv * KV_GROUP * t_new
    past = n_pages * PAGE_SIZE

    def k_map(i):
        return lambda b, c, pt: (pt[b, jnp.minimum(c, nch - 1) * pps + i], 0)

    def v_map(i):
        return lambda b, c, pt: (pt[b, jnp.maximum(c - nch, 0) * pps + i], 0)

    in_specs = [pl.BlockSpec((t_new, q.shape[1]), lambda b, c, pt: (b, 0)),
                pl.BlockSpec((t_new, kvw), lambda b, c, pt: (b, 0)),
                pl.BlockSpec((t_new, kvw), lambda b, c, pt: (b, 0))]
    in_specs += [pl.BlockSpec((PAGE_SIZE * n_kv, HEAD_DIM), k_map(i)) for i in range(pps)]
    in_specs += [pl.BlockSpec((PAGE_SIZE * n_kv, HEAD_DIM), v_map(i)) for i in range(pps)]
    body = functools.partial(_moba_sample_body, pps=pps, nch=nch, n_kv=n_kv, t_new=t_new,
                             n_pages=n_pages, scale=HEAD_DIM ** -0.5)
    grid_spec = pltpu.PrefetchScalarGridSpec(
        num_scalar_prefetch=1,
        grid=(batch, 2 * nch),
        in_specs=in_specs,
        out_specs=pl.BlockSpec((t_new, q.shape[1]), lambda b, c, pt: (b, 0)),
        scratch_shapes=[pltpu.VMEM((rows, kvw), BF16),
                        pltpu.VMEM((rows, past), F32),
                        pltpu.VMEM((rows, past), BF16),
                        pltpu.VMEM((n_pages // ppb, n_kv, HEAD_DIM), F32),
                        pltpu.VMEM((rows, kvw), F32),
                        pltpu.VMEM((rows, LANES), F32)],
    )
    return pl.pallas_call(
        body,
        grid_spec=grid_spec,
        out_shape=jax.ShapeDtypeStruct(q.shape, BF16),
        compiler_params=_params(("arbitrary", "arbitrary")),
        name="moba_sample",
    )(page_table, q, kn, vn, *([cache_k] * pps), *([cache_v] * pps))


def _hgrn_body(q_ref, lf_ref, k_ref, v_ref, gate_ref, gn_ref, h0_ref, o_ref, hout_ref, ht_sc,
               *, hb, chunk, sub, n_chunks, n_tsteps):
    t = pl.program_id(2)
    ns = chunk // sub

    @pl.when(t == 0)
    def _():
        for h in range(hb):
            ht_sc[h] = h0_ref[h].T

    r_i = lax.broadcasted_iota(jnp.int32, (chunk, chunk), 0)
    c_i = lax.broadcasted_iota(jnp.int32, (chunk, chunk), 1)
    causal = c_i <= r_i
    tri = jnp.where(causal, 1.0, 0.0).astype(BF16)
    width = hb * HEAD_DIM
    rowc = lax.broadcasted_iota(jnp.int32, (chunk, width), 0)
    heads = [slice(h * HEAD_DIM, (h + 1) * HEAD_DIM) for h in range(hb)]

    def one_chunk(ci, carry):
        rs = pl.ds(pl.multiple_of(ci * chunk, chunk), chunk)
        lf = lf_ref[rs, :]
        q = q_ref[rs, :]
        k = k_ref[rs, :]
        v = v_ref[rs, :]
        hi = lf.astype(BF16)
        lo = (lf - hi.astype(F32)).astype(BF16)
        bb = jnp.dot(tri, jnp.concatenate([hi, lo], axis=1), preferred_element_type=F32)
        b = bb[:, :width] + bb[:, width:]
        b_last = b[chunk - 1:chunk, :]
        qe = (q * jnp.exp(b)).astype(BF16)
        kd = (k * jnp.exp(b_last - b)).astype(BF16)
        decay = jnp.exp(b_last)
        refs = [jnp.zeros((1, width), F32)] + [b[sub * s - 1:sub * s, :] for s in range(1, ns)]
        gfull = jnp.concatenate([jnp.broadcast_to(r, (sub, width)) for r in refs], axis=0)
        qh = q * jnp.exp(b - gfull)
        qparts = [jnp.where((rowc >= sub * s) & (rowc < sub * (s + 1)), qh, 0.0).astype(BF16) for s in range(ns)]
        kparts = [jnp.where(rowc < sub * (s + 1), k * jnp.exp(refs[s] - b), 0.0).astype(BF16) for s in range(ns)]
        hts = [ht_sc[h] for h in range(hb)]
        o_state = [_nt_dot(qe[:, cs], hts[h].astype(BF16)) for h, cs in enumerate(heads)]
        att = [_nt_dot(jnp.concatenate([p[:, cs] for p in qparts], axis=1),
                       jnp.concatenate([p[:, cs] for p in kparts], axis=1)) for cs in heads]
        upd = [_tn_dot(v[:, cs], kd[:, cs]) for cs in heads]
        o_att = [jnp.dot(jnp.where(causal, att[h], 0.0).astype(BF16), v[:, cs], preferred_element_type=F32)
                 for h, cs in enumerate(heads)]
        for h, cs in enumerate(heads):
            ht_sc[h] = hts[h] * decay[:, cs] + upd[h]
            o = o_state[h] + o_att[h]
            y = o * lax.rsqrt(jnp.mean(o * o, axis=-1, keepdims=True) + EPS) * gn_ref[:, cs]
            o_ref[rs, cs] = (y * gate_ref[rs, cs]).astype(o_ref.dtype)
        return carry

    lax.fori_loop(0, n_chunks, one_chunk, 0)

    @pl.when(t == n_tsteps - 1)
    def _():
        for h in range(hb):
            hout_ref[h] = ht_sc[h].T


def _hgrn(hq, lf, hk, hv, gate, gnorm, h0, *, batch, seq):
    width = hq.shape[1]
    n_heads = width // HEAD_DIM
    hb = _tile(n_heads, 16)
    chunk = HG_CHUNK if seq % HG_CHUNK == 0 else seq
    sub = HG_SUB if chunk % HG_SUB == 0 else chunk
    tc = _tile(seq, 256, chunk)
    nt = seq // tc
    bw = hb * HEAD_DIM
    row_spec = pl.BlockSpec((tc, bw), lambda b, g, t: (b * nt + t, g))
    st_spec = pl.BlockSpec((None, hb, HEAD_DIM, HEAD_DIM), lambda b, g, t: (b, g, 0, 0))
    body = functools.partial(_hgrn_body, hb=hb, chunk=chunk, sub=sub, n_chunks=tc // chunk, n_tsteps=nt)
    return pl.pallas_call(
        body,
        grid=(batch, n_heads // hb, nt),
        in_specs=[row_spec, row_spec, row_spec, row_spec, row_spec,
                  pl.BlockSpec((1, bw), lambda b, g, t: (0, g)), st_spec],
        out_specs=[row_spec, st_spec],
        out_shape=[jax.ShapeDtypeStruct(hq.shape, BF16), jax.ShapeDtypeStruct(h0.shape, F32)],
        scratch_shapes=[pltpu.VMEM((hb, HEAD_DIM, HEAD_DIM), F32)],
        compiler_params=_params(("arbitrary", "arbitrary", "arbitrary")),
        name="hgrn",
    )(hq, lf, hk, hv, gate, gnorm, h0)


def _group_inputs(x, w, *, tm):
    d = x.shape[1]
    attn_w = d // 2
    kv_w = attn_w // KV_GROUP
    hg_w = d - attn_w
    xn = _rmsnorm(x, w["norm1_g"], _tile(x.shape[0], 256, 8))
    starts = [0, attn_w, attn_w + kv_w, attn_w + 2 * kv_w, attn_w + 2 * kv_w + hg_w,
              attn_w + 2 * kv_w + 2 * hg_w, attn_w + 2 * kv_w + 3 * hg_w]

    def proj(name, seg, n, epilogue, cols=(), outs=(), whole=False):
        tn = n if whole else _tile(n, 512, LANES)
        while starts[seg] % tn:
            tn = _tile(n, tn - 1, LANES)
        assert tn == n or not whole
        tm_call = _tile(xn.shape[0], min(tm, 512), 8) if whole else tm
        nk = 2 if whole and d % (2 * LANES) == 0 else 1
        return _matmul(name, [(xn, w["w_in"], 0)], n=n, col_off=starts[seg], tm=tm_call, tn=tn, nk=nk,
                       epilogue=epilogue, cols=cols, outs=outs)

    (q,) = proj("proj_q", 0, attn_w, _epi_head_norm, cols=(w["q_gain"],), outs=(BF16,))
    k, kb = proj("proj_k", 1, kv_w, _epi_head_norm, cols=(w["k_gain"],), outs=((F32, "heads"), BF16),
                 whole=True)
    v, vb = proj("proj_v", 2, kv_w, _epi_copy, outs=((F32, "heads"), BF16), whole=True)
    (hq,) = proj("proj_hq", 3, hg_w, _epi_silu, outs=(F32,))
    lf, hk = proj("proj_hf", 4, hg_w, functools.partial(_epi_forget, layer=0),
                  cols=(w["lb_logits"],), outs=(F32, F32))
    (hv,) = proj("proj_hi", 5, hg_w, _epi_copy, outs=(BF16,))
    (gate,) = proj("proj_hg", 6, hg_w, _epi_sigmoid, outs=(F32,))
    return dict(q=q, k=k, kb=kb, v=v, vb=vb, hq=hq, lf=lf, hk=hk, hv=hv, gate=gate)


def _group_outputs(x, attn_o, hg_o, w, *, tm):
    m, d = x.shape
    d_ff = w["w_up"].shape[1]
    tn = _tile(d, 512, LANES)
    (x1,) = _matmul("proj_out", [(attn_o, w["w_out"], 0), (hg_o, w["w_out"], attn_o.shape[1])], n=d, col_off=0,
                    tm=tm, tn=tn, nk=1, epilogue=_epi_residual, rows=(x,), outs=(F32,))
    hn = _rmsnorm(x1, w["norm2_g"], _tile(m, 256, 8))
    (act,) = _matmul("mlp_up", [(hn, w["w_up"], 0)], n=d_ff, col_off=0, tm=tm, tn=_tile(d_ff, 512, LANES),
                     nk=1, epilogue=_epi_relu2, outs=(BF16,))
    nk = d_ff // _tile(d_ff, 2048, LANES)
    (y,) = _matmul("mlp_down", [(act, w["w_down"], 0)], n=d, col_off=0, tm=tm, tn=_tile(d, 1024, LANES),
                   nk=nk, epilogue=_epi_residual, rows=(x1,), outs=(F32,))
    return y


def kernel(x_prompt, x_sample, cache_k, cache_v, state_h, page_table, norm1_g, w_in, q_norm_g, k_norm_g,
           lb_logits, hg_norm_g, w_out, norm2_g, w_up, w_down):
    depth = w_in.shape[0]
    assert depth == 1, "one layer"
    bp, tp, d = x_prompt.shape
    bs, ts, _ = x_sample.shape
    attn_w = d // 2
    n_q = attn_w // HEAD_DIM
    n_kv = n_q // KV_GROUP
    kv_w = n_kv * HEAD_DIM
    hg_w = d - attn_w
    n_hg = hg_w // HEAD_DIM

    w = dict(
        norm1_g=norm1_g[0], norm2_g=norm2_g[0],
        w_in=w_in[0], w_out=w_out[0], w_up=w_up[0], w_down=w_down[0].astype(BF16),
        q_gain=jnp.tile(q_norm_g[0], n_q).reshape(1, attn_w),
        k_gain=jnp.tile(k_norm_g[0], n_kv).reshape(1, kv_w),
        lb_logits=lb_logits,
    )
    gnorm = hg_norm_g[0].reshape(1, hg_w)

    xp = x_prompt.reshape(bp * tp, d)
    tm_p = _tile(bp * tp, 1024, 8)
    gp = _group_inputs(xp, w, tm=tm_p)
    attn_p = _moba_prompt(gp["q"], gp["kb"], gp["vb"], batch=bp, seq=tp)
    h0_p = jnp.zeros((bp, n_hg, HEAD_DIM, HEAD_DIM), F32)
    hg_p, h_p = _hgrn(gp["hq"], gp["lf"], gp["hk"], gp["hv"], gp["gate"], gnorm, h0_p, batch=bp, seq=tp)
    y_p = _group_outputs(xp, attn_p, hg_p, w, tm=tm_p)

    xs = x_sample.reshape(bs * ts, d)
    tm_s = _tile(bs * ts, 256, 8)
    gs = _group_inputs(xs, w, tm=tm_s)
    attn_s = _moba_sample(gs["q"], gs["kb"], gs["vb"], cache_k[0].reshape(-1, HEAD_DIM),
                          cache_v[0].reshape(-1, HEAD_DIM), page_table, batch=bs, t_new=ts)
    hg_s, h_s = _hgrn(gs["hq"], gs["lf"], gs["hk"], gs["hv"], gs["gate"], gnorm, state_h[0], batch=bs, seq=ts)
    y_s = _group_outputs(xs, attn_s, hg_s, w, tm=tm_s)

    return (y_p.reshape(bp, tp, d), y_s.reshape(bs, ts, d),
            gp["k"].reshape(1, bp, tp, n_kv, HEAD_DIM), gp["v"].reshape(1, bp, tp, n_kv, HEAD_DIM),
            h_p.reshape(1, bp, n_hg, HEAD_DIM, HEAD_DIM),
            gs["k"].reshape(1, bs, ts, n_kv, HEAD_DIM), gs["v"].reshape(1, bs, ts, n_kv, HEAD_DIM),
            h_s.reshape(1, bs, n_hg, HEAD_DIM, HEAD_DIM))
```

```python
import functools

import jax
import jax.numpy as jnp
from jax import lax
from jax.experimental import pallas as pl
from jax.experimental.pallas import tpu as pltpu

F32 = jnp.float32
BF16 = jnp.bfloat16

HEAD_DIM = 128
KV_GROUP = 2
PAGE_SIZE = 128
MOBA_BLOCK = 256
MOBA_TOPK = 3
HG_CHUNK = 64
HG_SUB = 16
EPS = 1e-6
NEG = -1e30
LANES = 128
SUB_N = 512
VMEM_LIMIT = 56 * 1024 * 1024


def _tile(n, target, mult=1):
    for t in range(min(n, target), 0, -1):
        if n % t == 0 and t % mult == 0:
            return t
    return n


def _params(sem):
    return pltpu.CompilerParams(dimension_semantics=sem, vmem_limit_bytes=VMEM_LIMIT)


def _nt_dot(a, b):
    return lax.dot_general(a, b, (((1,), (1,)), ((), ())), preferred_element_type=F32)


def _tn_dot(a, b):
    return lax.dot_general(a, b, (((0,), (0,)), ((), ())), preferred_element_type=F32)


def _rmsnorm_body(x_ref, g_ref, o_ref):
    x = x_ref[...]
    ms = jnp.mean(x * x, axis=-1, keepdims=True)
    o_ref[...] = (x * lax.rsqrt(ms + EPS) * g_ref[...]).astype(o_ref.dtype)


def _rmsnorm(x, g, tm):
    m, d = x.shape
    return pl.pallas_call(
        _rmsnorm_body,
        grid=(m // tm,),
        in_specs=[pl.BlockSpec((tm, d), lambda i: (i, 0)), pl.BlockSpec((1, d), lambda i: (0, 0))],
        out_specs=pl.BlockSpec((tm, d), lambda i: (i, 0)),
        out_shape=jax.ShapeDtypeStruct((m, d), BF16),
        compiler_params=_params(("arbitrary",)),
        name="rmsnorm",
    )(x, g.reshape(1, d))


def _col_view(ref, c0, width):
    if len(ref.shape) == 3:
        return ref.at[:, c0 // HEAD_DIM:(c0 + width) // HEAD_DIM, :]
    return ref.at[:, c0:c0 + width]


def _matmul_body(*refs, n_pairs, n_rows, n_cols, n_outs, nk, epilogue):
    xs = refs[0:2 * n_pairs:2]
    ws = refs[1:2 * n_pairs:2]
    p = 2 * n_pairs
    row_refs = refs[p:p + n_rows]
    col_refs = refs[p + n_rows:p + n_rows + n_cols]
    out_refs = refs[p + n_rows + n_cols:p + n_rows + n_cols + n_outs]
    tn = ws[0].shape[1]
    if nk == 1:
        sub = SUB_N if tn % SUB_N == 0 else tn
        for c0 in range(0, tn, sub):
            part = None
            for x_ref, w_ref in zip(xs, ws):
                d = jnp.dot(x_ref[...], w_ref[:, c0:c0 + sub].astype(BF16), preferred_element_type=F32)
                part = d if part is None else part + d
            epilogue(part, [_col_view(r, c0, sub) for r in row_refs], [_col_view(r, c0, sub) for r in col_refs],
                     [_col_view(r, c0, sub) for r in out_refs])
        return
    part = None
    for x_ref, w_ref in zip(xs, ws):
        d = jnp.dot(x_ref[...], w_ref[...].astype(BF16), preferred_element_type=F32)
        part = d if part is None else part + d
    acc_ref = refs[-1]
    k = pl.program_id(2)

    @pl.when(k == 0)
    def _():
        acc_ref[...] = part

    @pl.when(k > 0)
    def _():
        acc_ref[...] += part

    @pl.when(k == nk - 1)
    def _():
        epilogue(acc_ref[...], row_refs, col_refs, out_refs)


def _matmul(name, pairs, *, n, col_off, tm, tn, nk, epilogue, rows=(), cols=(), outs=()):
    m = pairs[0][0].shape[0]
    assert m % tm == 0 and n % tn == 0 and col_off % tn == 0
    joff = col_off // tn
    in_specs, args = [], []
    for x, w, row_off in pairs:
        kdim = x.shape[1]
        assert kdim % nk == 0
        tk = kdim // nk
        assert row_off % tk == 0
        in_specs += [pl.BlockSpec((tm, tk), lambda i, j, k: (i, k)),
                     pl.BlockSpec((tk, tn), lambda i, j, k, koff=row_off // tk: (k + koff, j + joff))]
        args += [x, w]
    for r in rows:
        in_specs.append(pl.BlockSpec((tm, tn), lambda i, j, k: (i, j)))
        args.append(r)
    for c in cols:
        in_specs.append(pl.BlockSpec((c.shape[0], tn), lambda i, j, k: (0, j)))
        args.append(c)
    out_specs, out_shape = [], []
    for o in outs:
        if isinstance(o, tuple):
            out_specs.append(pl.BlockSpec((tm, tn // HEAD_DIM, HEAD_DIM), lambda i, j, k: (i, j, 0)))
            out_shape.append(jax.ShapeDtypeStruct((m, n // HEAD_DIM, HEAD_DIM), o[0]))
        else:
            out_specs.append(pl.BlockSpec((tm, tn), lambda i, j, k: (i, j)))
            out_shape.append(jax.ShapeDtypeStruct((m, n), o))
    body = functools.partial(_matmul_body, n_pairs=len(pairs), n_rows=len(rows), n_cols=len(cols),
                             n_outs=len(outs), nk=nk, epilogue=epilogue)
    return pl.pallas_call(
        body,
        grid=(m // tm, n // tn, nk),
        in_specs=in_specs,
        out_specs=out_specs,
        out_shape=out_shape,
        scratch_shapes=[pltpu.VMEM((tm, tn), F32)] if nk > 1 else [],
        compiler_params=_params(("arbitrary", "arbitrary", "arbitrary")),
        name=name,
    )(*args)


def _store_head(o, h, y):
    if len(o.shape) == 3:
        o[:, h, :] = y.astype(o.dtype)
    else:
        o[:, h * HEAD_DIM:(h + 1) * HEAD_DIM] = y.astype(o.dtype)


def _epi_head_norm(acc, rows, cols, outs):
    g = cols[0][...]
    for h in range(acc.shape[1] // HEAD_DIM):
        sl = slice(h * HEAD_DIM, (h + 1) * HEAD_DIM)
        blk = acc[:, sl]
        y = blk * lax.rsqrt(jnp.mean(blk * blk, axis=-1, keepdims=True) + EPS) * g[:, sl]
        for o in outs:
            _store_head(o, h, y)


def _epi_copy(acc, rows, cols, outs):
    for h in range(acc.shape[1] // HEAD_DIM):
        for o in outs:
            _store_head(o, h, acc[:, h * HEAD_DIM:(h + 1) * HEAD_DIM])


def _epi_silu(acc, rows, cols, outs):
    outs[0][...] = (acc * jax.nn.sigmoid(acc)).astype(outs[0].dtype)


def _epi_sigmoid(acc, rows, cols, outs):
    outs[0][...] = jax.nn.sigmoid(acc).astype(outs[0].dtype)


def _epi_forget(acc, rows, cols, outs, *, layer):
    logits = cols[0][...]
    e = jnp.exp(logits - jnp.max(logits, axis=0, keepdims=True))
    lb = jnp.sum(e[:layer + 1], axis=0, keepdims=True) / jnp.sum(e, axis=0, keepdims=True)
    forget = lb + (1.0 - lb) * jax.nn.sigmoid(acc)
    outs[0][...] = jnp.log(forget)
    outs[1][...] = 1.0 - forget


def _epi_residual(acc, rows, cols, outs):
    outs[0][...] = rows[0][...] + acc


def _epi_relu2(acc, rows, cols, outs):
    r = jnp.maximum(acc, 0.0)
    outs[0][...] = (r * r).astype(outs[0].dtype)


def _select_blocks_t(sc, n_valid, topk):
    nb = sc.shape[0]
    row = lax.broadcasted_iota(jnp.int32, sc.shape, 0)
    valid = row < n_valid
    sel = jnp.zeros(sc.shape, F32)
    for j in range(nb):
        sj = sc[j:j + 1, :]
        beats = ((sc > sj) | ((sc == sj) & (row < j))) & valid
        rank = jnp.sum(jnp.where(beats, 1.0, 0.0), axis=0, keepdims=True)
        chosen = jnp.where((rank < float(topk)) & (j < n_valid), 1.0, 0.0)
        sel = jnp.where(row == j, chosen, sel)
    return sel


def _moba_prompt_tile(i, q_ref, k_ref, vt_sc, o_ref, kmean, *, scale):
    blk = MOBA_BLOCK
    rows = KV_GROUP * blk
    qb = q_ref[i * blk:(i + 1) * blk, :]
    q2 = jnp.concatenate([qb[:, g * HEAD_DIM:(g + 1) * HEAD_DIM] for g in range(KV_GROUP)], axis=0)
    sel_t = _select_blocks_t(_nt_dot(kmean, q2), i, MOBA_TOPK)
    bias_t = jnp.where(sel_t > 0.5, 0.0, NEG)

    def scores(j):
        s = _nt_dot(k_ref[j * blk:(j + 1) * blk, :], q2)
        if j == i:
            kpos = lax.broadcasted_iota(jnp.int32, s.shape, 0)
            qpos = lax.broadcasted_iota(jnp.int32, s.shape, 1) % blk
            return jnp.where(kpos <= qpos, s, NEG)
        return s + bias_t[j:j + 1, :]

    blocks = list(range(i + 1))
    mrun = scores(i)
    for j in blocks[:-1]:
        mrun = jnp.maximum(mrun, scores(j))
    m = jnp.max(mrun, axis=0, keepdims=True)
    lsum = None
    acc = None
    for j in blocks:
        pe = jnp.exp((scores(j) - m) * scale)
        lsum = pe if lsum is None else lsum + pe
        d = jnp.dot(vt_sc[:, j * blk:(j + 1) * blk], pe.astype(BF16), preferred_element_type=F32)
        acc = d if acc is None else acc + d
    o = (acc / jnp.sum(lsum, axis=0, keepdims=True)).T
    for g in range(KV_GROUP):
        o_ref[i * blk:(i + 1) * blk, g * HEAD_DIM:(g + 1) * HEAD_DIM] = o[g * blk:(g + 1) * blk].astype(o_ref.dtype)


def _moba_prompt_body(q_ref, k_ref, v_ref, o_ref, kmean_sc, vt_sc, *, nb, scale):
    p = pl.program_id(2)

    @pl.when(p == 0)
    def _():
        kall = k_ref[...].astype(F32)
        kmean_sc[...] = jnp.sum(kall.reshape(nb, MOBA_BLOCK, HEAD_DIM), axis=1) * (1.0 / MOBA_BLOCK)
        vt_sc[...] = v_ref[...].astype(F32).T.astype(BF16)

    for c in range(nb // 2):
        @pl.when(p == c)
        def _(c=c):
            kmean = kmean_sc[...].astype(BF16)
            for i in (c, nb - 1 - c):
                _moba_prompt_tile(i, q_ref, k_ref, vt_sc, o_ref, kmean, scale=scale)


def _moba_prompt(q, kb, vb, *, batch, seq):
    assert seq % (2 * MOBA_BLOCK) == 0, "query tiles are processed in (p, nb-1-p) pairs"
    nb = seq // MOBA_BLOCK
    assert nb <= LANES
    n_kv = kb.shape[1] // HEAD_DIM
    gw = KV_GROUP * HEAD_DIM
    body = functools.partial(_moba_prompt_body, nb=nb, scale=HEAD_DIM ** -0.5)
    return pl.pallas_call(
        body,
        grid=(batch, n_kv, nb // 2),
        in_specs=[pl.BlockSpec((seq, gw), lambda b, n, p: (b, n)),
                  pl.BlockSpec((seq, HEAD_DIM), lambda b, n, p: (b, n)),
                  pl.BlockSpec((seq, HEAD_DIM), lambda b, n, p: (b, n))],
        out_specs=pl.BlockSpec((seq, gw), lambda b, n, p: (b, n)),
        out_shape=jax.ShapeDtypeStruct(q.shape, BF16),
        scratch_shapes=[pltpu.VMEM((nb, HEAD_DIM), F32),
                        pltpu.VMEM((HEAD_DIM, seq), BF16)],
        compiler_params=_params(("arbitrary", "arbitrary", "arbitrary")),
        name="moba_prompt",
    )(q, kb, vb)


def _moba_sample_body(pt_ref, q_ref, kn_ref, vn_ref, *rest, pps, nch, n_kv, t_new, n_pages, scale):
    k_refs = rest[:pps]
    v_refs = rest[pps:2 * pps]
    o_ref = rest[2 * pps]
    qbd_sc, s_sc, p_sc, km_sc, oacc_sc, l_sc = rest[2 * pps + 1:]
    c = pl.program_id(1)
    ppb = MOBA_BLOCK // PAGE_SIZE
    nb = n_pages // ppb
    gt = KV_GROUP * t_new
    rows = n_kv * gt
    kvw = n_kv * HEAD_DIM

    @pl.when(c == 0)
    def _():
        qv = q_ref[...].astype(F32)
        per_g = [jnp.concatenate([qv[:, (KV_GROUP * n + g) * HEAD_DIM:(KV_GROUP * n + g + 1) * HEAD_DIM]
                                  for n in range(n_kv)], axis=1) for g in range(KV_GROUP)]
        tiled = jnp.concatenate([per_g[g] for n in range(n_kv) for g in range(KV_GROUP)], axis=0)
        rown = lax.broadcasted_iota(jnp.int32, (rows, kvw), 0) // gt
        coln = lax.broadcasted_iota(jnp.int32, (rows, kvw), 1) // HEAD_DIM
        qbd_sc[...] = jnp.where(rown == coln, tiled, 0.0).astype(BF16)

    def page_rows(ref):
        n_tok = ref.shape[0] // n_kv
        return jnp.concatenate([ref[pl.ds(n, n_tok, stride=n_kv), :] for n in range(n_kv)], axis=1)

    @pl.when(c < nch)
    def _():
        qbd = qbd_sc[...]
        colsum = None
        for i in range(pps):
            cs = jnp.sum(k_refs[i][...].reshape(PAGE_SIZE, n_kv, HEAD_DIM), axis=0)
            colsum = cs if i % ppb == 0 else colsum + cs
            if i % ppb == ppb - 1:
                km_sc[pl.ds(c * (pps // ppb) + i // ppb, 1)] = colsum[None]
            col0 = pl.multiple_of((c * pps + i) * PAGE_SIZE, PAGE_SIZE)
            s_sc[:, pl.ds(col0, PAGE_SIZE)] = _nt_dot(qbd, page_rows(k_refs[i]).astype(BF16))

    @pl.when(c == nch - 1)
    def _():
        qbd = qbd_sc[...]
        kmean = (jnp.concatenate([km_sc[:, n, :] for n in range(n_kv)], axis=1)
                 * (1.0 / MOBA_BLOCK)).astype(BF16)
        sc_t = _nt_dot(kmean, qbd)
        sel_t = _select_blocks_t(sc_t, nb, MOBA_TOPK)
        bias = jnp.where(sel_t > 0.5, 0.0, NEG).T
        zpad = jnp.zeros((LANES - t_new, kvw), F32)
        knp = jnp.concatenate([kn_ref[...].astype(F32), zpad], axis=0).astype(BF16)
        vnp = jnp.concatenate([vn_ref[...].astype(F32), zpad], axis=0).astype(BF16)
        s_own = _nt_dot(qbd, knp)
        tok = lax.broadcasted_iota(jnp.int32, s_own.shape, 0) % t_new
        key = lax.broadcasted_iota(jnp.int32, s_own.shape, 1)
        s_own = jnp.where(key <= tok, s_own, NEG)
        m = jnp.max(s_own, axis=-1, keepdims=True)
        for j in range(nb):
            sl = slice(j * MOBA_BLOCK, (j + 1) * MOBA_BLOCK)
            sb = s_sc[:, sl] + bias[:, j:j + 1]
            s_sc[:, sl] = sb
            m = jnp.maximum(m, jnp.max(sb, axis=-1, keepdims=True))
        p_own = jnp.exp((s_own - m) * scale)
        l = jnp.sum(p_own, axis=-1, keepdims=True)
        for j in range(nb):
            sl = slice(j * MOBA_BLOCK, (j + 1) * MOBA_BLOCK)
            p = jnp.exp((s_sc[:, sl] - m) * scale)
            l = l + jnp.sum(p, axis=-1, keepdims=True)
            p_sc[:, sl] = p.astype(BF16)
        l_sc[...] = jnp.broadcast_to(l, l_sc.shape)
        oacc_sc[...] = jnp.dot(p_own.astype(BF16), vnp, preferred_element_type=F32)

    @pl.when(c >= nch)
    def _():
        acc = oacc_sc[...]
        for i in range(pps):
            col0 = pl.multiple_of(((c - nch) * pps + i) * PAGE_SIZE, PAGE_SIZE)
            acc = acc + jnp.dot(p_sc[:, pl.ds(col0, PAGE_SIZE)], page_rows(v_refs[i]).astype(BF16),
                                preferred_element_type=F32)
        oacc_sc[...] = acc

    @pl.when(c == 2 * nch - 1)
    def _():
        o = oacc_sc[...] / l_sc[:, :1]
        for n in range(n_kv):
            for g in range(KV_GROUP):
                r0 = n * gt + g * t_new
                h = KV_GROUP * n + g
                o_ref[:, h * HEAD_DIM:(h + 1) * HEAD_DIM] = (
                    o[r0:r0 + t_new, n * HEAD_DIM:(n + 1) * HEAD_DIM].astype(o_ref.dtype))


def _moba_sample(q, kn, vn, cache_k, cache_v, page_table, *, batch, t_new):
    n_pages = page_table.shape[1]
    ppb = MOBA_BLOCK // PAGE_SIZE
    assert n_pages % ppb == 0 and n_pages // ppb >= MOBA_TOPK
    kvw = kn.shape[1]
    n_kv = kvw // HEAD_DIM
    pps = _tile(n_pages, 16, ppb)
    nch = n_pages // pps
    rows = n_kv * KV_GROUP * t_new
    past = n_pages * PAGE_SIZE

    def k_map(i):
        return lambda b, c, pt: (pt[b, jnp.minimum(c, nch - 1) * pps + i], 0)

    def v_map(i):
        return lambda b, c, pt: (pt[b, jnp.maximum(c - nch, 0) * pps + i], 0)

    in_specs = [pl.BlockSpec((t_new, q.shape[1]), lambda b, c, pt: (b, 0)),
                pl.BlockSpec((t_new, kvw), lambda b, c, pt: (b, 0)),
                pl.BlockSpec((t_new, kvw), lambda b, c, pt: (b, 0))]
    in_specs += [pl.BlockSpec((PAGE_SIZE * n_kv, HEAD_DIM), k_map(i)) for i in range(pps)]
    in_specs += [pl.BlockSpec((PAGE_SIZE * n_kv, HEAD_DIM), v_map(i)) for i in range(pps)]
    body = functools.partial(_moba_sample_body, pps=pps, nch=nch, n_kv=n_kv, t_new=t_new,
                             n_pages=n_pages, scale=HEAD_DIM ** -0.5)
    grid_spec = pltpu.PrefetchScalarGridSpec(
        num_scalar_prefetch=1,
        grid=(batch, 2 * nch),
        in_specs=in_specs,
        out_specs=pl.BlockSpec((t_new, q.shape[1]), lambda b, c, pt: (b, 0)),
        scratch_shapes=[pltpu.VMEM((rows, kvw), BF16),
                        pltpu.VMEM((rows, past), F32),
                        pltpu.VMEM((rows, past), BF16),
                        pltpu.VMEM((n_pages // ppb, n_kv, HEAD_DIM), F32),
                        pltpu.VMEM((rows, kvw), F32),
                        pltpu.VMEM((rows, LANES), F32)],
    )
    return pl.pallas_call(
        body,
        grid_spec=grid_spec,
        out_shape=jax.ShapeDtypeStruct(q.shape, BF16),
        compiler_params=_params(("arbitrary", "arbitrary")),
        name="moba_sample",
    )(page_table, q, kn, vn, *([cache_k] * pps), *([cache_v] * pps))


def _hgrn_body(q_ref, lf_ref, k_ref, v_ref, gate_ref, gn_ref, h0_ref, o_ref, hout_ref, ht_sc,
               *, hb, chunk, sub, n_chunks, n_tsteps):
    t = pl.program_id(2)
    ns = chunk // sub

    @pl.when(t == 0)
    def _():
        for h in range(hb):
            ht_sc[h] = h0_ref[h].T

    r_i = lax.broadcasted_iota(jnp.int32, (chunk, chunk), 0)
    c_i = lax.broadcasted_iota(jnp.int32, (chunk, chunk), 1)
    causal = c_i <= r_i
    tri = jnp.where(causal, 1.0, 0.0).astype(BF16)
    width = hb * HEAD_DIM
    rowc = lax.broadcasted_iota(jnp.int32, (chunk, width), 0)
    heads = [slice(h * HEAD_DIM, (h + 1) * HEAD_DIM) for h in range(hb)]

    def one_chunk(ci, carry):
        rs = pl.ds(pl.multiple_of(ci * chunk, chunk), chunk)
        lf = lf_ref[rs, :]
        q = q_ref[rs, :]
        k = k_ref[rs, :]
        v = v_ref[rs, :]
        hi = lf.astype(BF16)
        lo = (lf - hi.astype(F32)).astype(BF16)
        bb = jnp.dot(tri, jnp.concatenate([hi, lo], axis=1), preferred_element_type=F32)
        b = bb[:, :width] + bb[:, width:]
        b_last = b[chunk - 1:chunk, :]
        qe = (q * jnp.exp(b)).astype(BF16)
        kd = (k * jnp.exp(b_last - b)).astype(BF16)
        decay = jnp.exp(b_last)
        refs = [jnp.zeros((1, width), F32)] + [b[sub * s - 1:sub * s, :] for s in range(1, ns)]
        gfull = jnp.concatenate([jnp.broadcast_to(r, (sub, width)) for r in refs], axis=0)
        qh = q * jnp.exp(b - gfull)
        qparts = [jnp.where((rowc >= sub * s) & (rowc < sub * (s + 1)), qh, 0.0).astype(BF16) for s in range(ns)]
        kparts = [jnp.where(rowc < sub * (s + 1), k * jnp.exp(refs[s] - b), 0.0).astype(BF16) for s in range(ns)]
        hts = [ht_sc[h] for h in range(hb)]
        o_state = [_nt_dot(qe[:, cs], hts[h].astype(BF16)) for h, cs in enumerate(heads)]
        att = [_nt_dot(jnp.concatenate([p[:, cs] for p in qparts], axis=1),
                       jnp.concatenate([p[:, cs] for p in kparts], axis=1)) for cs in heads]
        upd = [_tn_dot(v[:, cs], kd[:, cs]) for cs in heads]
        o_att = [jnp.dot(jnp.where(causal, att[h], 0.0).astype(BF16), v[:, cs], preferred_element_type=F32)
                 for h, cs in enumerate(heads)]
        for h, cs in enumerate(heads):
            ht_sc[h] = hts[h] * decay[:, cs] + upd[h]
            o = o_state[h] + o_att[h]
            y = o * lax.rsqrt(jnp.mean(o * o, axis=-1, keepdims=True) + EPS) * gn_ref[:, cs]
            o_ref[rs, cs] = (y * gate_ref[rs, cs]).astype(o_ref.dtype)
        return carry

    lax.fori_loop(0, n_chunks, one_chunk, 0)

    @pl.when(t == n_tsteps - 1)
    def _():
        for h in range(hb):
            hout_ref[h] = ht_sc[h].T


def _hgrn(hq, lf, hk, hv, gate, gnorm, h0, *, batch, seq):
    width = hq.shape[1]
    n_heads = width // HEAD_DIM
    hb = _tile(n_heads, 16)
    chunk = HG_CHUNK if seq % HG_CHUNK == 0 else seq
    sub = HG_SUB if chunk % HG_SUB == 0 else chunk
    tc = _tile(seq, 256, chunk)
    nt = seq // tc
    bw = hb * HEAD_DIM
    row_spec = pl.BlockSpec((tc, bw), lambda b, g, t: (b * nt + t, g))
    st_spec = pl.BlockSpec((None, hb, HEAD_DIM, HEAD_DIM), lambda b, g, t: (b, g, 0, 0))
    body = functools.partial(_hgrn_body, hb=hb, chunk=chunk, sub=sub, n_chunks=tc // chunk, n_tsteps=nt)
    return pl.pallas_call(
        body,
        grid=(batch, n_heads // hb, nt),
        in_specs=[row_spec, row_spec, row_spec, row_spec, row_spec,
                  pl.BlockSpec((1, bw), lambda b, g, t: (0, g)), st_spec],
        out_specs=[row_spec, st_spec],
        out_shape=[jax.ShapeDtypeStruct(hq.shape, BF16), jax.ShapeDtypeStruct(h0.shape, F32)],
        scratch_shapes=[pltpu.VMEM((hb, HEAD_DIM, HEAD_DIM), F32)],
        compiler_params=_params(("arbitrary", "arbitrary", "arbitrary")),
        name="hgrn",
    )(hq, lf, hk, hv, gate, gnorm, h0)


def _group_inputs(x, w, *, tm):
    d = x.shape[1]
    attn_w = d // 2
    kv_w = attn_w // KV_GROUP
    hg_w = d - attn_w
    xn = _rmsnorm(x, w["norm1_g"], _tile(x.shape[0], 256, 8))
    starts = [0, attn_w, attn_w + kv_w, attn_w + 2 * kv_w, attn_w + 2 * kv_w + hg_w,
              attn_w + 2 * kv_w + 2 * hg_w, attn_w + 2 * kv_w + 3 * hg_w]

    def proj(name, seg, n, epilogue, cols=(), outs=(), whole=False):
        tn = n if whole else _tile(n, 1024 if len(outs) == 1 else 512, LANES)
        while starts[seg] % tn:
            tn = _tile(n, tn - 1, LANES)
        assert tn == n or not whole
        tm_call = _tile(xn.shape[0], min(tm, 512), 8) if whole else tm
        return _matmul(name, [(xn, w["w_in"], 0)], n=n, col_off=starts[seg], tm=tm_call, tn=tn, nk=1,
                       epilogue=epilogue, cols=cols, outs=outs)

    (q,) = proj("proj_q", 0, attn_w, _epi_head_norm, cols=(w["q_gain"],), outs=(BF16,))
    k, kb = proj("proj_k", 1, kv_w, _epi_head_norm, cols=(w["k_gain"],), outs=((F32, "heads"), BF16),
                 whole=True)
    v, vb = proj("proj_v", 2, kv_w, _epi_copy, outs=((F32, "heads"), BF16), whole=True)
    (hq,) = proj("proj_hq", 3, hg_w, _epi_silu, outs=(F32,))
    lf, hk = proj("proj_hf", 4, hg_w, functools.partial(_epi_forget, layer=0),
                  cols=(w["lb_logits"],), outs=(F32, F32))
    (hv,) = proj("proj_hi", 5, hg_w, _epi_copy, outs=(BF16,))
    (gate,) = proj("proj_hg", 6, hg_w, _epi_sigmoid, outs=(F32,))
    return dict(q=q, k=k, kb=kb, v=v, vb=vb, hq=hq, lf=lf, hk=hk, hv=hv, gate=gate)


def _group_outputs(x, attn_o, hg_o, w, *, tm):
    m, d = x.shape
    d_ff = w["w_up"].shape[1]
    tn = _tile(d, 512, LANES)
    (x1,) = _matmul("proj_out", [(attn_o, w["w_out"], 0), (hg_o, w["w_out"], attn_o.shape[1])], n=d, col_off=0,
                    tm=tm, tn=tn, nk=1, epilogue=_epi_residual, rows=(x,), outs=(F32,))
    hn = _rmsnorm(x1, w["norm2_g"], _tile(m, 256, 8))
    (act,) = _matmul("mlp_up", [(hn, w["w_up"], 0)], n=d_ff, col_off=0, tm=tm, tn=_tile(d_ff, 512, LANES),
                     nk=1, epilogue=_epi_relu2, outs=(BF16,))
    nk = d_ff // _tile(d_ff, 2048, LANES)
    (y,) = _matmul("mlp_down", [(act, w["w_down"], 0)], n=d, col_off=0, tm=tm, tn=_tile(d, 1024, LANES),
                   nk=nk, epilogue=_epi_residual, rows=(x1,), outs=(F32,))
    return y


def kernel(x_prompt, x_sample, cache_k, cache_v, state_h, page_table, norm1_g, w_in, q_norm_g, k_norm_g,
           lb_logits, hg_norm_g, w_out, norm2_g, w_up, w_down):
    depth = w_in.shape[0]
    assert depth == 1, "one layer"
    bp, tp, d = x_prompt.shape
    bs, ts, _ = x_sample.shape
    attn_w = d // 2
    n_q = attn_w // HEAD_DIM
    n_kv = n_q // KV_GROUP
    kv_w = n_kv * HEAD_DIM
    hg_w = d - attn_w
    n_hg = hg_w // HEAD_DIM

    w = dict(
        norm1_g=norm1_g[0], norm2_g=norm2_g[0],
        w_in=w_in[0].astype(BF16), w_out=w_out[0], w_up=w_up[0], w_down=w_down[0].astype(BF16),
        q_gain=jnp.tile(q_norm_g[0], n_q).reshape(1, attn_w),
        k_gain=jnp.tile(k_norm_g[0], n_kv).reshape(1, kv_w),
        lb_logits=lb_logits,
    )
    gnorm = hg_norm_g[0].reshape(1, hg_w)

    xp = x_prompt.reshape(bp * tp, d)
    tm_p = _tile(bp * tp, 1024, 8)
    gp = _group_inputs(xp, w, tm=tm_p)
    attn_p = _moba_prompt(gp["q"], gp["kb"], gp["vb"], batch=bp, seq=tp)
    h0_p = jnp.zeros((bp, n_hg, HEAD_DIM, HEAD_DIM), F32)
    hg_p, h_p = _hgrn(gp["hq"], gp["lf"], gp["hk"], gp["hv"], gp["gate"], gnorm, h0_p, batch=bp, seq=tp)
    y_p = _group_outputs(xp, attn_p, hg_p, w, tm=tm_p)

    xs = x_sample.reshape(bs * ts, d)
    tm_s = _tile(bs * ts, 256, 8)
    gs = _group_inputs(xs, w, tm=tm_s)
    attn_s = _moba_sample(gs["q"], gs["kb"], gs["vb"], cache_k[0].reshape(-1, HEAD_DIM),
                          cache_v[0].reshape(-1, HEAD_DIM), page_table, batch=bs, t_new=ts)
    hg_s, h_s = _hgrn(gs["hq"], gs["lf"], gs["hk"], gs["hv"], gs["gate"], gnorm, state_h[0], batch=bs, seq=ts)
    y_s = _group_outputs(xs, attn_s, hg_s, w, tm=tm_s)

    return (y_p.reshape(bp, tp, d), y_s.reshape(bs, ts, d),
            gp["k"].reshape(1, bp, tp, n_kv, HEAD_DIM), gp["v"].reshape(1, bp, tp, n_kv, HEAD_DIM),
            h_p.reshape(1, bp, n_hg, HEAD_DIM, HEAD_DIM),
            gs["k"].reshape(1, bs, ts, n_kv, HEAD_DIM), gs["v"].reshape(1, bs, ts, n_kv, HEAD_DIM),
            h_s.reshape(1, bs, n_hg, HEAD_DIM, HEAD_DIM))
```

```python
import functools

import jax
import jax.numpy as jnp
from jax import lax
from jax.experimental import pallas as pl
from jax.experimental.pallas import tpu as pltpu

F32 = jnp.float32
BF16 = jnp.bfloat16

HEAD_DIM = 128
KV_GROUP = 2
PAGE_SIZE = 128
MOBA_BLOCK = 256
MOBA_TOPK = 3
HG_CHUNK = 64
HG_SUB = 16
EPS = 1e-6
NEG = -1e30
LANES = 128
SUB_N = 512
MLP_DOWN_TK = 2048
VMEM_LIMIT = 60 * 1024 * 1024


def _tile(n, target, mult=1):
    for t in range(min(n, target), 0, -1):
        if n % t == 0 and t % mult == 0:
            return t
    return n


def _params(sem):
    return pltpu.CompilerParams(dimension_semantics=sem, vmem_limit_bytes=VMEM_LIMIT)


def _nt_dot(a, b):
    return lax.dot_general(a, b, (((1,), (1,)), ((), ())), preferred_element_type=F32)


def _tn_dot(a, b):
    return lax.dot_general(a, b, (((0,), (0,)), ((), ())), preferred_element_type=F32)


def _rmsnorm_body(x_ref, g_ref, o_ref):
    x = x_ref[...]
    ms = jnp.mean(x * x, axis=-1, keepdims=True)
    o_ref[...] = (x * lax.rsqrt(ms + EPS) * g_ref[...]).astype(o_ref.dtype)


def _rmsnorm(x, g, tm):
    m, d = x.shape
    return pl.pallas_call(
        _rmsnorm_body,
        grid=(m // tm,),
        in_specs=[pl.BlockSpec((tm, d), lambda i: (i, 0)), pl.BlockSpec((1, d), lambda i: (0, 0))],
        out_specs=pl.BlockSpec((tm, d), lambda i: (i, 0)),
        out_shape=jax.ShapeDtypeStruct((m, d), BF16),
        compiler_params=_params(("arbitrary",)),
        name="rmsnorm",
    )(x, g.reshape(1, d))


def _col_view(ref, c0, width):
    if len(ref.shape) == 3:
        return ref.at[:, c0 // HEAD_DIM:(c0 + width) // HEAD_DIM, :]
    return ref.at[:, c0:c0 + width]


def _matmul_tile(xs, ws, row_refs, col_refs, out_refs, acc_ref, *, nk, epilogue):
    tn = ws[0].shape[1]
    if nk == 1:
        sub = SUB_N if tn % SUB_N == 0 else tn
        for c0 in range(0, tn, sub):
            part = None
            for x_ref, w_ref in zip(xs, ws):
                d = jnp.dot(x_ref[...], w_ref[:, c0:c0 + sub].astype(BF16), preferred_element_type=F32)
                part = d if part is None else part + d
            epilogue(part, [_col_view(r, c0, sub) for r in row_refs], [_col_view(r, c0, sub) for r in col_refs],
                     [_col_view(r, c0, sub) for r in out_refs])
        return
    part = None
    for x_ref, w_ref in zip(xs, ws):
        d = jnp.dot(x_ref[...], w_ref[...].astype(BF16), preferred_element_type=F32)
        part = d if part is None else part + d
    k = pl.program_id(2)

    @pl.when(k == 0)
    def _():
        acc_ref[...] = part

    @pl.when(k > 0)
    def _():
        acc_ref[...] += part

    @pl.when(k == nk - 1)
    def _():
        epilogue(acc_ref[...], row_refs, col_refs, out_refs)


def _matmul_body(*refs, n_pairs, n_rows, n_cols, n_outs, nk, epilogue, rider, acc_in_out):
    it = iter(refs)
    take = lambda cnt: [next(it) for _ in range(cnt)]
    xw = take(2 * n_pairs)
    xs, ws = xw[0::2], xw[1::2]
    row_refs, col_refs = take(n_rows), take(n_cols)
    xs2, row_refs2 = (take(n_pairs), take(n_rows)) if rider else ([], [])
    out_refs = take(n_outs)
    out_refs2 = take(n_outs) if rider else []
    acc_ref = out_refs[0] if acc_in_out else (next(it) if nk > 1 else None)
    acc_ref2 = next(it) if rider and nk > 1 else None
    _matmul_tile(xs, ws, row_refs, col_refs, out_refs, acc_ref, nk=nk, epilogue=epilogue)
    if rider:
        @pl.when(pl.program_id(0) == 0)
        def _():
            _matmul_tile(xs2, ws, row_refs2, col_refs, out_refs2, acc_ref2, nk=nk, epilogue=epilogue)


def _matmul(name, pairs, *, n, col_off, tm, tn, nk, epilogue, rows=(), cols=(), outs=(), rider=None,
            acc_in_out=False):
    m = pairs[0][0].shape[0]
    assert m % tm == 0 and n % tn == 0 and col_off % tn == 0
    joff = col_off // tn
    nj = n // tn
    first_j = lambda i, j: jnp.where(i == 0, j, nj - 1)
    first_k = lambda i, k: jnp.where(i == 0, k, nk - 1)
    in_specs, args = [], []
    for x, w, row_off in pairs:
        kdim = x.shape[1]
        assert kdim % nk == 0
        tk = kdim // nk
        assert row_off % tk == 0
        in_specs += [pl.BlockSpec((tm, tk), lambda i, j, k: (i, k)),
                     pl.BlockSpec((tk, tn), lambda i, j, k, koff=row_off // tk: (k + koff, j + joff))]
        args += [x, w]
    for r in rows:
        in_specs.append(pl.BlockSpec((tm, tn), lambda i, j, k: (i, j)))
        args.append(r)
    for c in cols:
        in_specs.append(pl.BlockSpec((c.shape[0], tn), lambda i, j, k: (0, j)))
        args.append(c)
    m2 = 0
    if rider:
        xs2, rows2 = rider
        m2 = xs2[0].shape[0]
        for x2, (x, _, _) in zip(xs2, pairs):
            assert x2.shape[1] == x.shape[1]
            in_specs.append(pl.BlockSpec((m2, x.shape[1] // nk), lambda i, j, k: (0, first_k(i, k))))
            args.append(x2)
        for r in rows2:
            in_specs.append(pl.BlockSpec((m2, tn), lambda i, j, k: (0, first_j(i, j))))
            args.append(r)
    out_specs, out_shape = [], []

    def add_outs(mrows, trows, imap, jmap):
        for o in outs:
            if isinstance(o, tuple):
                out_specs.append(pl.BlockSpec((trows, tn // HEAD_DIM, HEAD_DIM),
                                              lambda i, j, k: (imap(i), jmap(i, j), 0)))
                out_shape.append(jax.ShapeDtypeStruct((mrows, n // HEAD_DIM, HEAD_DIM), o[0]))
            else:
                out_specs.append(pl.BlockSpec((trows, tn), lambda i, j, k: (imap(i), jmap(i, j))))
                out_shape.append(jax.ShapeDtypeStruct((mrows, n), o))

    add_outs(m, tm, lambda i: i, lambda i, j: j)
    if rider:
        add_outs(m2, m2, lambda i: 0, first_j)
    scratch = []
    if nk > 1:
        assert not acc_in_out or (len(outs) == 1 and outs[0] == F32)
        scratch = ([] if acc_in_out else [pltpu.VMEM((tm, tn), F32)]) + ([pltpu.VMEM((m2, tn), F32)] if rider else [])
    body = functools.partial(_matmul_body, n_pairs=len(pairs), n_rows=len(rows), n_cols=len(cols),
                             n_outs=len(outs), nk=nk, epilogue=epilogue, rider=bool(rider),
                             acc_in_out=acc_in_out and nk > 1)
    return pl.pallas_call(
        body,
        grid=(m // tm, nj, nk),
        in_specs=in_specs,
        out_specs=out_specs,
        out_shape=out_shape,
        scratch_shapes=scratch,
        compiler_params=_params(("arbitrary", "arbitrary", "arbitrary")),
        name=name,
    )(*args)


def _store_head(o, h, y):
    if len(o.shape) == 3:
        o[:, h, :] = y.astype(o.dtype)
    else:
        o[:, h * HEAD_DIM:(h + 1) * HEAD_DIM] = y.astype(o.dtype)


def _epi_head_norm(acc, rows, cols, outs):
    g = cols[0][...]
    for h in range(acc.shape[1] // HEAD_DIM):
        sl = slice(h * HEAD_DIM, (h + 1) * HEAD_DIM)
        blk = acc[:, sl]
        y = blk * lax.rsqrt(jnp.mean(blk * blk, axis=-1, keepdims=True) + EPS) * g[:, sl]
        for o in outs:
            _store_head(o, h, y)


def _epi_copy(acc, rows, cols, outs):
    for h in range(acc.shape[1] // HEAD_DIM):
        for o in outs:
            _store_head(o, h, acc[:, h * HEAD_DIM:(h + 1) * HEAD_DIM])


def _epi_silu(acc, rows, cols, outs):
    outs[0][...] = (acc * jax.nn.sigmoid(acc)).astype(outs[0].dtype)


def _epi_sigmoid(acc, rows, cols, outs):
    outs[0][...] = jax.nn.sigmoid(acc).astype(outs[0].dtype)


def _epi_forget(acc, rows, cols, outs, *, layer):
    logits = cols[0][...]
    e = jnp.exp(logits - jnp.max(logits, axis=0, keepdims=True))
    lb = jnp.sum(e[:layer + 1], axis=0, keepdims=True) / jnp.sum(e, axis=0, keepdims=True)
    forget = lb + (1.0 - lb) * jax.nn.sigmoid(acc)
    outs[0][...] = jnp.log(forget)
    outs[1][...] = 1.0 - forget


def _epi_residual(acc, rows, cols, outs):
    outs[0][...] = rows[0][...] + acc


def _epi_relu2(acc, rows, cols, outs):
    r = jnp.maximum(acc, 0.0)
    outs[0][...] = (r * r).astype(outs[0].dtype)


def _select_blocks_t(sc, n_valid, topk):
    nb = sc.shape[0]
    row = lax.broadcasted_iota(jnp.int32, sc.shape, 0)
    valid = row < n_valid
    sel = jnp.zeros(sc.shape, F32)
    for j in range(nb):
        sj = sc[j:j + 1, :]
        beats = ((sc > sj) | ((sc == sj) & (row < j))) & valid
        rank = jnp.sum(jnp.where(beats, 1.0, 0.0), axis=0, keepdims=True)
        chosen = jnp.where((rank < float(topk)) & (j < n_valid), 1.0, 0.0)
        sel = jnp.where(row == j, chosen, sel)
    return sel


def _moba_prompt_tile(i, q_ref, k_ref, vt_sc, o_ref, kmean, *, scale):
    blk = MOBA_BLOCK
    rows = KV_GROUP * blk
    qb = q_ref[i * blk:(i + 1) * blk, :]
    q2 = jnp.concatenate([qb[:, g * HEAD_DIM:(g + 1) * HEAD_DIM] for g in range(KV_GROUP)], axis=0)
    sel_t = _select_blocks_t(_nt_dot(kmean, q2), i, MOBA_TOPK)
    bias_t = jnp.where(sel_t > 0.5, 0.0, NEG)

    def scores(j):
        s = _nt_dot(k_ref[j * blk:(j + 1) * blk, :], q2)
        if j == i:
            kpos = lax.broadcasted_iota(jnp.int32, s.shape, 0)
            qpos = lax.broadcasted_iota(jnp.int32, s.shape, 1) % blk
            return jnp.where(kpos <= qpos, s, NEG)
        return s + bias_t[j:j + 1, :]

    blocks = list(range(i + 1))
    mrun = scores(i)
    for j in blocks[:-1]:
        mrun = jnp.maximum(mrun, scores(j))
    m = jnp.max(mrun, axis=0, keepdims=True)
    lsum = None
    acc = None
    for j in blocks:
        pe = jnp.exp((scores(j) - m) * scale)
        lsum = pe if lsum is None else lsum + pe
        d = jnp.dot(vt_sc[:, j * blk:(j + 1) * blk], pe.astype(BF16), preferred_element_type=F32)
        acc = d if acc is None else acc + d
    o = (acc / jnp.sum(lsum, axis=0, keepdims=True)).T
    for g in range(KV_GROUP):
        o_ref[i * blk:(i + 1) * blk, g * HEAD_DIM:(g + 1) * HEAD_DIM] = o[g * blk:(g + 1) * blk].astype(o_ref.dtype)


def _moba_prompt_body(q_ref, k_ref, v_ref, o_ref, kmean_sc, vt_sc, *, nb, scale):
    p = pl.program_id(2)

    @pl.when(p == 0)
    def _():
        kall = k_ref[...].astype(F32)
        kmean_sc[...] = jnp.sum(kall.reshape(nb, MOBA_BLOCK, HEAD_DIM), axis=1) * (1.0 / MOBA_BLOCK)
        vt_sc[...] = v_ref[...].astype(F32).T.astype(BF16)

    for c in range(nb // 2):
        @pl.when(p == c)
        def _(c=c):
            kmean = kmean_sc[...].astype(BF16)
            for i in (c, nb - 1 - c):
                _moba_prompt_tile(i, q_ref, k_ref, vt_sc, o_ref, kmean, scale=scale)


def _moba_prompt(q, kb, vb, *, batch, seq):
    assert seq % (2 * MOBA_BLOCK) == 0, "query tiles are processed in (p, nb-1-p) pairs"
    nb = seq // MOBA_BLOCK
    assert nb <= LANES
    n_kv = kb.shape[1] // HEAD_DIM
    gw = KV_GROUP * HEAD_DIM
    body = functools.partial(_moba_prompt_body, nb=nb, scale=HEAD_DIM ** -0.5)
    return pl.pallas_call(
        body,
        grid=(batch, n_kv, nb // 2),
        in_specs=[pl.BlockSpec((seq, gw), lambda b, n, p: (b, n)),
                  pl.BlockSpec((seq, HEAD_DIM), lambda b, n, p: (b, n)),
                  pl.BlockSpec((seq, HEAD_DIM), lambda b, n, p: (b, n))],
        out_specs=pl.BlockSpec((seq, gw), lambda b, n, p: (b, n)),
        out_shape=jax.ShapeDtypeStruct(q.shape, BF16),
        scratch_shapes=[pltpu.VMEM((nb, HEAD_DIM), F32),
                        pltpu.VMEM((HEAD_DIM, seq), BF16)],
        compiler_params=_params(("arbitrary", "arbitrary", "arbitrary")),
        name="moba_prompt",
    )(q, kb, vb)


def _moba_sample_body(pt_ref, q_ref, kn_ref, vn_ref, *rest, pps, nch, n_kv, t_new, n_pages, scale):
    k_refs = rest[:pps]
    v_refs = rest[pps:2 * pps]
    o_ref = rest[2 * pps]
    qbd_sc, s_sc, p_sc, km_sc, oacc_sc, l_sc = rest[2 * pps + 1:]
    c = pl.program_id(1)
    ppb = MOBA_BLOCK // PAGE_SIZE
    nb = n_pages // ppb
    gt = KV_GROUP * t_new
    rows = n_kv * gt
    kvw = n_kv * HEAD_DIM

    @pl.when(c == 0)
    def _():
        qv = q_ref[...].astype(F32)
        per_g = [jnp.concatenate([qv[:, (KV_GROUP * n + g) * HEAD_DIM:(KV_GROUP * n + g + 1) * HEAD_DIM]
                                  for n in range(n_kv)], axis=1) for g in range(KV_GROUP)]
        tiled = jnp.concatenate([per_g[g] for n in range(n_kv) for g in range(KV_GROUP)], axis=0)
        rown = lax.broadcasted_iota(jnp.int32, (rows, kvw), 0) // gt
        coln = lax.broadcasted_iota(jnp.int32, (rows, kvw), 1) // HEAD_DIM
        qbd_sc[...] = jnp.where(rown == coln, tiled, 0.0).astype(BF16)

    def page_rows(ref):
        n_tok = ref.shape[0] // n_kv
        return jnp.concatenate([ref[pl.ds(n, n_tok, stride=n_kv), :] for n in range(n_kv)], axis=1)

    @pl.when(c < nch)
    def _():
        qbd = qbd_sc[...]
        colsum = None
        for i in range(pps):
            cs = jnp.sum(k_refs[i][...].reshape(PAGE_SIZE, n_kv, HEAD_DIM), axis=0)
            colsum = cs if i % ppb == 0 else colsum + cs
            if i % ppb == ppb - 1:
                km_sc[pl.ds(c * (pps // ppb) + i // ppb, 1)] = colsum[None]
            col0 = pl.multiple_of((c * pps + i) * PAGE_SIZE, PAGE_SIZE)
            s_sc[:, pl.ds(col0, PAGE_SIZE)] = _nt_dot(qbd, page_rows(k_refs[i]).astype(BF16))

    @pl.when(c == nch - 1)
    def _():
        qbd = qbd_sc[...]
        kmean = (jnp.concatenate([km_sc[:, n, :] for n in range(n_kv)], axis=1)
                 * (1.0 / MOBA_BLOCK)).astype(BF16)
        sc_t = _nt_dot(kmean, qbd)
        sel_t = _select_blocks_t(sc_t, nb, MOBA_TOPK)
        bias = jnp.where(sel_t > 0.5, 0.0, NEG).T
        zpad = jnp.zeros((LANES - t_new, kvw), F32)
        knp = jnp.concatenate([kn_ref[...].astype(F32), zpad], axis=0).astype(BF16)
        vnp = jnp.concatenate([vn_ref[...].astype(F32), zpad], axis=0).astype(BF16)
        s_own = _nt_dot(qbd, knp)
        tok = lax.broadcasted_iota(jnp.int32, s_own.shape, 0) % t_new
        key = lax.broadcasted_iota(jnp.int32, s_own.shape, 1)
        s_own = jnp.where(key <= tok, s_own, NEG)
        m = jnp.max(s_own, axis=-1, keepdims=True)
        for j in range(nb):
            sl = slice(j * MOBA_BLOCK, (j + 1) * MOBA_BLOCK)
            sb = s_sc[:, sl] + bias[:, j:j + 1]
            s_sc[:, sl] = sb
            m = jnp.maximum(m, jnp.max(sb, axis=-1, keepdims=True))
        p_own = jnp.exp((s_own - m) * scale)
        l = jnp.sum(p_own, axis=-1, keepdims=True)
        for j in range(nb):
            sl = slice(j * MOBA_BLOCK, (j + 1) * MOBA_BLOCK)
            p = jnp.exp((s_sc[:, sl] - m) * scale)
            l = l + jnp.sum(p, axis=-1, keepdims=True)
            p_sc[:, sl] = p.astype(BF16)
        l_sc[...] = jnp.broadcast_to(l, l_sc.shape)
        oacc_sc[...] = jnp.dot(p_own.astype(BF16), vnp, preferred_element_type=F32)

    @pl.when(c >= nch)
    def _():
        acc = oacc_sc[...]
        for i in range(pps):
            col0 = pl.multiple_of(((c - nch) * pps + i) * PAGE_SIZE, PAGE_SIZE)
            acc = acc + jnp.dot(p_sc[:, pl.ds(col0, PAGE_SIZE)], page_rows(v_refs[i]).astype(BF16),
                                preferred_element_type=F32)
        oacc_sc[...] = acc

    @pl.when(c == 2 * nch - 1)
    def _():
        o = oacc_sc[...] / l_sc[:, :1]
        for n in range(n_kv):
            for g in range(KV_GROUP):
                r0 = n * gt + g * t_new
                h = KV_GROUP * n + g
                o_ref[:, h * HEAD_DIM:(h + 1) * HEAD_DIM] = (
                    o[r0:r0 + t_new, n * HEAD_DIM:(n + 1) * HEAD_DIM].astype(o_ref.dtype))


def _moba_sample(q, kn, vn, cache_k, cache_v, page_table, *, batch, t_new):
    n_pages = page_table.shape[1]
    ppb = MOBA_BLOCK // PAGE_SIZE
    assert n_pages % ppb == 0 and n_pages // ppb >= MOBA_TOPK
    kvw = kn.shape[1]
    n_kv = kvw // HEAD_DIM
    pps = _tile(n_pages, 16, ppb)
    nch = n_pages // pps
    rows = n_kv * KV_GROUP * t_new
    past = n_pages * PAGE_SIZE

    def k_map(i):
        return lambda b, c, pt: (pt[b, jnp.minimum(c, nch - 1) * pps + i], 0)

    def v_map(i):
        return lambda b, c, pt: (pt[b, jnp.maximum(c - nch, 0) * pps + i], 0)

    in_specs = [pl.BlockSpec((t_new, q.shape[1]), lambda b, c, pt: (b, 0)),
                pl.BlockSpec((t_new, kvw), lambda b, c, pt: (b, 0)),
                pl.BlockSpec((t_new, kvw), lambda b, c, pt: (b, 0))]
    in_specs += [pl.BlockSpec((PAGE_SIZE * n_kv, HEAD_DIM), k_map(i)) for i in range(pps)]
    in_specs += [pl.BlockSpec((PAGE_SIZE * n_kv, HEAD_DIM), v_map(i)) for i in range(pps)]
    body = functools.partial(_moba_sample_body, pps=pps, nch=nch, n_kv=n_kv, t_new=t_new,
                             n_pages=n_pages, scale=HEAD_DIM ** -0.5)
    grid_spec = pltpu.PrefetchScalarGridSpec(
        num_scalar_prefetch=1,
        grid=(batch, 2 * nch),
        in_specs=in_specs,
        out_specs=pl.BlockSpec((t_new, q.shape[1]), lambda b, c, pt: (b, 0)),
        scratch_shapes=[pltpu.VMEM((rows, kvw), BF16),
                        pltpu.VMEM((rows, past), F32),
                        pltpu.VMEM((rows, past), BF16),
                        pltpu.VMEM((n_pages // ppb, n_kv, HEAD_DIM), F32),
                        pltpu.VMEM((rows, kvw), F32),
                        pltpu.VMEM((rows, LANES), F32)],
    )
    return pl.pallas_call(
        body,
        grid_spec=grid_spec,
        out_shape=jax.ShapeDtypeStruct(q.shape, BF16),
        compiler_params=_params(("arbitrary", "arbitrary")),
        name="moba_sample",
    )(page_table, q, kn, vn, *([cache_k] * pps), *([cache_v] * pps))


def _hgrn_body(q_ref, lf_ref, k_ref, v_ref, gate_ref, gn_ref, h0_ref, o_ref, hout_ref, ht_sc,
               *, hb, chunk, sub, n_chunks, n_tsteps):
    t = pl.program_id(2)
    ns = chunk // sub

    @pl.when(t == 0)
    def _():
        for h in range(hb):
            ht_sc[h] = h0_ref[h].T

    r_i = lax.broadcasted_iota(jnp.int32, (chunk, chunk), 0)
    c_i = lax.broadcasted_iota(jnp.int32, (chunk, chunk), 1)
    causal = c_i <= r_i
    tri = jnp.where(causal, 1.0, 0.0).astype(BF16)
    width = hb * HEAD_DIM
    rowc = lax.broadcasted_iota(jnp.int32, (chunk, width), 0)
    heads = [slice(h * HEAD_DIM, (h + 1) * HEAD_DIM) for h in range(hb)]

    def one_chunk(ci, carry):
        rs = pl.ds(pl.multiple_of(ci * chunk, chunk), chunk)
        lf = lf_ref[rs, :]
        q = q_ref[rs, :]
        k = k_ref[rs, :]
        v = v_ref[rs, :]
        hi = lf.astype(BF16)
        lo = (lf - hi.astype(F32)).astype(BF16)
        bb = jnp.dot(tri, jnp.concatenate([hi, lo], axis=1), preferred_element_type=F32)
        b = bb[:, :width] + bb[:, width:]
        b_last = b[chunk - 1:chunk, :]
        qe = (q * jnp.exp(b)).astype(BF16)
        kd = (k * jnp.exp(b_last - b)).astype(BF16)
        decay = jnp.exp(b_last)
        refs = [jnp.zeros((1, width), F32)] + [b[sub * s - 1:sub * s, :] for s in range(1, ns)]
        gfull = jnp.concatenate([jnp.broadcast_to(r, (sub, width)) for r in refs], axis=0)
        qh = q * jnp.exp(b - gfull)
        qparts = [jnp.where((rowc >= sub * s) & (rowc < sub * (s + 1)), qh, 0.0).astype(BF16) for s in range(ns)]
        kparts = [jnp.where(rowc < sub * (s + 1), k * jnp.exp(refs[s] - b), 0.0).astype(BF16) for s in range(ns)]
        hts = [ht_sc[h] for h in range(hb)]
        o_state = [_nt_dot(qe[:, cs], hts[h].astype(BF16)) for h, cs in enumerate(heads)]
        att = [_nt_dot(jnp.concatenate([p[:, cs] for p in qparts], axis=1),
                       jnp.concatenate([p[:, cs] for p in kparts], axis=1)) for cs in heads]
        upd = [_tn_dot(v[:, cs], kd[:, cs]) for cs in heads]
        o_att = [jnp.dot(jnp.where(causal, att[h], 0.0).astype(BF16), v[:, cs], preferred_element_type=F32)
                 for h, cs in enumerate(heads)]
        for h, cs in enumerate(heads):
            ht_sc[h] = hts[h] * decay[:, cs] + upd[h]
            o = o_state[h] + o_att[h]
            y = o * lax.rsqrt(jnp.mean(o * o, axis=-1, keepdims=True) + EPS) * gn_ref[:, cs]
            o_ref[rs, cs] = (y * gate_ref[rs, cs]).astype(o_ref.dtype)
        return carry

    lax.fori_loop(0, n_chunks, one_chunk, 0)

    @pl.when(t == n_tsteps - 1)
    def _():
        for h in range(hb):
            hout_ref[h] = ht_sc[h].T


def _hgrn(hq, lf, hk, hv, gate, gnorm, h0, *, batch, seq):
    width = hq.shape[1]
    n_heads = width // HEAD_DIM
    hb = _tile(n_heads, 16)
    chunk = HG_CHUNK if seq % HG_CHUNK == 0 else seq
    sub = HG_SUB if chunk % HG_SUB == 0 else chunk
    tc = _tile(seq, 256, chunk)
    nt = seq // tc
    bw = hb * HEAD_DIM
    row_spec = pl.BlockSpec((tc, bw), lambda b, g, t: (b * nt + t, g))
    st_spec = pl.BlockSpec((None, hb, HEAD_DIM, HEAD_DIM), lambda b, g, t: (b, g, 0, 0))
    body = functools.partial(_hgrn_body, hb=hb, chunk=chunk, sub=sub, n_chunks=tc // chunk, n_tsteps=nt)
    return pl.pallas_call(
        body,
        grid=(batch, n_heads // hb, nt),
        in_specs=[row_spec, row_spec, row_spec, row_spec, row_spec,
                  pl.BlockSpec((1, bw), lambda b, g, t: (0, g)), st_spec],
        out_specs=[row_spec, st_spec],
        out_shape=[jax.ShapeDtypeStruct(hq.shape, BF16), jax.ShapeDtypeStruct(h0.shape, F32)],
        scratch_shapes=[pltpu.VMEM((hb, HEAD_DIM, HEAD_DIM), F32)],
        compiler_params=_params(("arbitrary", "arbitrary", "arbitrary")),
        name="hgrn",
    )(hq, lf, hk, hv, gate, gnorm, h0)


_IN_NAMES = ("q", "k", "kb", "v", "vb", "hq", "lf", "hk", "hv", "gate")


def _layer_inputs(xp, xs, w, *, tm):
    d = xp.shape[1]
    attn_w = d // 2
    kv_w = attn_w // KV_GROUP
    hg_w = d - attn_w
    xn_p = _rmsnorm(xp, w["norm1_g"], _tile(xp.shape[0], 256, 8))
    xn_s = _rmsnorm(xs, w["norm1_g"], _tile(xs.shape[0], 256, 8))
    starts = [0, attn_w, attn_w + kv_w, attn_w + 2 * kv_w, attn_w + 2 * kv_w + hg_w,
              attn_w + 2 * kv_w + 2 * hg_w, attn_w + 2 * kv_w + 3 * hg_w]

    def proj(name, seg, n, epilogue, cols=(), outs=(), whole=False):
        tn = n if whole else _tile(n, 1024 if len(outs) == 1 else 512, LANES)
        while starts[seg] % tn:
            tn = _tile(n, tn - 1, LANES)
        assert tn == n or not whole
        tm_call = _tile(xn_p.shape[0], min(tm, 512), 8) if whole else tm
        return _matmul(name, [(xn_p, w["w_in"], 0)], n=n, col_off=starts[seg], tm=tm_call, tn=tn, nk=1,
                       epilogue=epilogue, cols=cols, outs=outs, rider=([xn_s], []))

    res = []
    res += proj("proj_q", 0, attn_w, _epi_head_norm, cols=(w["q_gain"],), outs=(BF16,))
    res += proj("proj_k", 1, kv_w, _epi_head_norm, cols=(w["k_gain"],), outs=((F32, "heads"), BF16), whole=True)
    res += proj("proj_v", 2, kv_w, _epi_copy, outs=((F32, "heads"), BF16), whole=True)
    res += proj("proj_hq", 3, hg_w, _epi_silu, outs=(F32,))
    res += proj("proj_hf", 4, hg_w, functools.partial(_epi_forget, layer=0), cols=(w["lb_logits"],),
                outs=(F32, F32))
    res += proj("proj_hi", 5, hg_w, _epi_copy, outs=(BF16,))
    res += proj("proj_hg", 6, hg_w, _epi_sigmoid, outs=(F32,))
    counts = (1, 2, 2, 1, 2, 1, 1)
    prompt, sample, at = [], [], 0
    for c in counts:
        prompt += res[at:at + c]
        sample += res[at + c:at + 2 * c]
        at += 2 * c
    return dict(zip(_IN_NAMES, prompt)), dict(zip(_IN_NAMES, sample))


def _layer_outputs(xp, xs, attn_p, attn_s, hg_p, hg_s, w, *, tm):
    d = xp.shape[1]
    d_ff = w["w_up"].shape[1]
    aw = attn_p.shape[1]
    x1_p, x1_s = _matmul("proj_out", [(attn_p, w["w_out"], 0), (hg_p, w["w_out"], aw)], n=d, col_off=0, tm=tm,
                         tn=_tile(d, 512, LANES), nk=1, epilogue=_epi_residual, rows=(xp,), outs=(F32,),
                         rider=([attn_s, hg_s], [xs]))
    hn_p = _rmsnorm(x1_p, w["norm2_g"], _tile(xp.shape[0], 256, 8))
    hn_s = _rmsnorm(x1_s, w["norm2_g"], _tile(xs.shape[0], 256, 8))
    act_p, act_s = _matmul("mlp_up", [(hn_p, w["w_up"], 0)], n=d_ff, col_off=0, tm=tm, tn=_tile(d_ff, 512, LANES),
                           nk=1, epilogue=_epi_relu2, outs=(BF16,), rider=([hn_s], []))
    nk = d_ff // _tile(d_ff, MLP_DOWN_TK, LANES)
    y_p, y_s = _matmul("mlp_down", [(act_p, w["w_down"], 0)], n=d, col_off=0, tm=tm, tn=_tile(d, 1024, LANES),
                       nk=nk, epilogue=_epi_residual, rows=(x1_p,), outs=(F32,), rider=([act_s], [x1_s]),
                       acc_in_out=True)
    return y_p, y_s


def kernel(x_prompt, x_sample, cache_k, cache_v, state_h, page_table, norm1_g, w_in, q_norm_g, k_norm_g,
           lb_logits, hg_norm_g, w_out, norm2_g, w_up, w_down):
    depth = w_in.shape[0]
    assert depth == 1, "one layer"
    bp, tp, d = x_prompt.shape
    bs, ts, _ = x_sample.shape
    attn_w = d // 2
    n_q = attn_w // HEAD_DIM
    n_kv = n_q // KV_GROUP
    kv_w = n_kv * HEAD_DIM
    hg_w = d - attn_w
    n_hg = hg_w // HEAD_DIM

    w = dict(
        norm1_g=norm1_g[0], norm2_g=norm2_g[0],
        w_in=w_in[0].astype(BF16), w_out=w_out[0], w_up=w_up[0], w_down=w_down[0].astype(BF16),
        q_gain=jnp.tile(q_norm_g[0], n_q).reshape(1, attn_w),
        k_gain=jnp.tile(k_norm_g[0], n_kv).reshape(1, kv_w),
        lb_logits=lb_logits,
    )
    gnorm = hg_norm_g[0].reshape(1, hg_w)

    xp = x_prompt.reshape(bp * tp, d)
    xs = x_sample.reshape(bs * ts, d)
    tm = _tile(bp * tp, 1024, 8)
    gp, gs = _layer_inputs(xp, xs, w, tm=tm)

    attn_p = _moba_prompt(gp["q"], gp["kb"], gp["vb"], batch=bp, seq=tp)
    h0_p = jnp.zeros((bp, n_hg, HEAD_DIM, HEAD_DIM), F32)
    hg_p, h_p = _hgrn(gp["hq"], gp["lf"], gp["hk"], gp["hv"], gp["gate"], gnorm, h0_p, batch=bp, seq=tp)
    attn_s = _moba_sample(gs["q"], gs["kb"], gs["vb"], cache_k[0].reshape(-1, HEAD_DIM),
                          cache_v[0].reshape(-1, HEAD_DIM), page_table, batch=bs, t_new=ts)
    hg_s, h_s = _hgrn(gs["hq"], gs["lf"], gs["hk"], gs["hv"], gs["gate"], gnorm, state_h[0], batch=bs, seq=ts)

    y_p, y_s = _layer_outputs(xp, xs, attn_p, attn_s, hg_p, hg_s, w, tm=tm)

    return (y_p.reshape(bp, tp, d), y_s.reshape(bs, ts, d),
            gp["k"].reshape(1, bp, tp, n_kv, HEAD_DIM), gp["v"].reshape(1, bp, tp, n_kv, HEAD_DIM),
            h_p.reshape(1, bp, n_hg, HEAD_DIM, HEAD_DIM),
            gs["k"].reshape(1, bs, ts, n_kv, HEAD_DIM), gs["v"].reshape(1, bs, ts, n_kv, HEAD_DIM),
            h_s.reshape(1, bs, n_hg, HEAD_DIM, HEAD_DIM))
```

```python
import functools

import jax
import jax.numpy as jnp
from jax import lax
from jax.experimental import pallas as pl
from jax.experimental.pallas import tpu as pltpu

F32 = jnp.float32
BF16 = jnp.bfloat16

HEAD_DIM = 128
KV_GROUP = 2
PAGE_SIZE = 128
MOBA_BLOCK = 256
MOBA_TOPK = 3
HG_CHUNK = 64
HG_SUB = 16
EPS = 1e-6
NEG = -1e30
LOG2E = 1.4426950408889634
LANES = 128
SUB_N = 512
MLP_DOWN_TK = 2048
KV_TM = 1024
VMEM_LIMIT = 60 * 1024 * 1024


def _tile(n, target, mult=1):
    for t in range(min(n, target), 0, -1):
        if n % t == 0 and t % mult == 0:
            return t
    return n


def _params(sem):
    return pltpu.CompilerParams(dimension_semantics=sem, vmem_limit_bytes=VMEM_LIMIT)


def _nt_dot(a, b):
    return lax.dot_general(a, b, (((1,), (1,)), ((), ())), preferred_element_type=F32)


def _tn_dot(a, b):
    return lax.dot_general(a, b, (((0,), (0,)), ((), ())), preferred_element_type=F32)


def _rmsnorm_body(x_ref, g_ref, o_ref):
    x = x_ref[...]
    ms = jnp.mean(x * x, axis=-1, keepdims=True)
    o_ref[...] = (x * lax.rsqrt(ms + EPS) * g_ref[...]).astype(o_ref.dtype)


def _rmsnorm(x, g, tm):
    m, d = x.shape
    return pl.pallas_call(
        _rmsnorm_body,
        grid=(m // tm,),
        in_specs=[pl.BlockSpec((tm, d), lambda i: (i, 0)), pl.BlockSpec((1, d), lambda i: (0, 0))],
        out_specs=pl.BlockSpec((tm, d), lambda i: (i, 0)),
        out_shape=jax.ShapeDtypeStruct((m, d), BF16),
        compiler_params=_params(("arbitrary",)),
        name="rmsnorm",
    )(x, g.reshape(1, d))


def _col_view(ref, c0, width):
    if len(ref.shape) == 3:
        return ref.at[:, c0 // HEAD_DIM:(c0 + width) // HEAD_DIM, :]
    return ref.at[:, c0:c0 + width]


def _matmul_tile(xs, ws, row_refs, col_refs, out_refs, acc_ref, *, nk, epilogue):
    tn = ws[0].shape[1]
    if nk == 1:
        sub = SUB_N if tn % SUB_N == 0 else tn
        for c0 in range(0, tn, sub):
            part = None
            for x_ref, w_ref in zip(xs, ws):
                d = jnp.dot(x_ref[...], w_ref[:, c0:c0 + sub].astype(BF16), preferred_element_type=F32)
                part = d if part is None else part + d
            epilogue(part, [_col_view(r, c0, sub) for r in row_refs], [_col_view(r, c0, sub) for r in col_refs],
                     [_col_view(r, c0, sub) for r in out_refs])
        return
    part = None
    for x_ref, w_ref in zip(xs, ws):
        d = jnp.dot(x_ref[...], w_ref[...].astype(BF16), preferred_element_type=F32)
        part = d if part is None else part + d
    k = pl.program_id(2)

    @pl.when(k == 0)
    def _():
        acc_ref[...] = part

    @pl.when(k > 0)
    def _():
        acc_ref[...] += part

    @pl.when(k == nk - 1)
    def _():
        epilogue(acc_ref[...], row_refs, col_refs, out_refs)


def _matmul_body(*refs, n_pairs, n_rows, n_cols, n_outs, nk, epilogue, rider, acc_in_out):
    it = iter(refs)
    take = lambda cnt: [next(it) for _ in range(cnt)]
    xw = take(2 * n_pairs)
    xs, ws = xw[0::2], xw[1::2]
    row_refs, col_refs = take(n_rows), take(n_cols)
    xs2, row_refs2 = (take(n_pairs), take(n_rows)) if rider else ([], [])
    out_refs = take(n_outs)
    out_refs2 = take(n_outs) if rider else []
    acc_ref = out_refs[0] if acc_in_out else (next(it) if nk > 1 else None)
    acc_ref2 = next(it) if rider and nk > 1 else None
    _matmul_tile(xs, ws, row_refs, col_refs, out_refs, acc_ref, nk=nk, epilogue=epilogue)
    if rider:
        @pl.when(pl.program_id(0) == 0)
        def _():
            _matmul_tile(xs2, ws, row_refs2, col_refs, out_refs2, acc_ref2, nk=nk, epilogue=epilogue)


def _matmul(name, pairs, *, n, col_off, tm, tn, nk, epilogue, rows=(), cols=(), outs=(), rider=None,
            acc_in_out=False):
    m = pairs[0][0].shape[0]
    assert m % tm == 0 and n % tn == 0 and col_off % tn == 0
    joff = col_off // tn
    nj = n // tn
    first_j = lambda i, j: jnp.where(i == 0, j, nj - 1)
    first_k = lambda i, k: jnp.where(i == 0, k, nk - 1)
    in_specs, args = [], []
    for x, w, row_off in pairs:
        kdim = x.shape[1]
        assert kdim % nk == 0
        tk = kdim // nk
        assert row_off % tk == 0
        in_specs += [pl.BlockSpec((tm, tk), lambda i, j, k: (i, k)),
                     pl.BlockSpec((tk, tn), lambda i, j, k, koff=row_off // tk: (k + koff, j + joff))]
        args += [x, w]
    for r in rows:
        in_specs.append(pl.BlockSpec((tm, tn), lambda i, j, k: (i, j)))
        args.append(r)
    for c in cols:
        in_specs.append(pl.BlockSpec((c.shape[0], tn), lambda i, j, k: (0, j)))
        args.append(c)
    m2 = 0
    if rider:
        xs2, rows2 = rider
        m2 = xs2[0].shape[0]
        for x2, (x, _, _) in zip(xs2, pairs):
            assert x2.shape[1] == x.shape[1]
            in_specs.append(pl.BlockSpec((m2, x.shape[1] // nk), lambda i, j, k: (0, first_k(i, k))))
            args.append(x2)
        for r in rows2:
            in_specs.append(pl.BlockSpec((m2, tn), lambda i, j, k: (0, first_j(i, j))))
            args.append(r)
    out_specs, out_shape = [], []

    def add_outs(mrows, trows, imap, jmap):
        for o in outs:
            if isinstance(o, tuple):
                out_specs.append(pl.BlockSpec((trows, tn // HEAD_DIM, HEAD_DIM),
                                              lambda i, j, k: (imap(i), jmap(i, j), 0)))
                out_shape.append(jax.ShapeDtypeStruct((mrows, n // HEAD_DIM, HEAD_DIM), o[0]))
            else:
                out_specs.append(pl.BlockSpec((trows, tn), lambda i, j, k: (imap(i), jmap(i, j))))
                out_shape.append(jax.ShapeDtypeStruct((mrows, n), o))

    add_outs(m, tm, lambda i: i, lambda i, j: j)
    if rider:
        add_outs(m2, m2, lambda i: 0, first_j)
    scratch = []
    if nk > 1:
        assert not acc_in_out or (len(outs) == 1 and outs[0] == F32)
        scratch = ([] if acc_in_out else [pltpu.VMEM((tm, tn), F32)]) + ([pltpu.VMEM((m2, tn), F32)] if rider else [])
    body = functools.partial(_matmul_body, n_pairs=len(pairs), n_rows=len(rows), n_cols=len(cols),
                             n_outs=len(outs), nk=nk, epilogue=epilogue, rider=bool(rider),
                             acc_in_out=acc_in_out and nk > 1)
    return pl.pallas_call(
        body,
        grid=(m // tm, nj, nk),
        in_specs=in_specs,
        out_specs=out_specs,
        out_shape=out_shape,
        scratch_shapes=scratch,
        compiler_params=_params(("arbitrary", "arbitrary", "arbitrary")),
        name=name,
    )(*args)


def _store_head(o, h, y):
    if len(o.shape) == 3:
        o[:, h, :] = y.astype(o.dtype)
    else:
        o[:, h * HEAD_DIM:(h + 1) * HEAD_DIM] = y.astype(o.dtype)


def _epi_head_norm(acc, rows, cols, outs):
    g = cols[0][...]
    for h in range(acc.shape[1] // HEAD_DIM):
        sl = slice(h * HEAD_DIM, (h + 1) * HEAD_DIM)
        blk = acc[:, sl]
        y = blk * lax.rsqrt(jnp.mean(blk * blk, axis=-1, keepdims=True) + EPS) * g[:, sl]
        for o in outs:
            _store_head(o, h, y)


def _epi_copy(acc, rows, cols, outs):
    for h in range(acc.shape[1] // HEAD_DIM):
        for o in outs:
            _store_head(o, h, acc[:, h * HEAD_DIM:(h + 1) * HEAD_DIM])


def _epi_silu(acc, rows, cols, outs):
    outs[0][...] = (acc * jax.nn.sigmoid(acc)).astype(outs[0].dtype)


def _epi_sigmoid(acc, rows, cols, outs):
    outs[0][...] = jax.nn.sigmoid(acc).astype(outs[0].dtype)


def _epi_forget(acc, rows, cols, outs, *, layer):
    logits = cols[0][...]
    e = jnp.exp(logits - jnp.max(logits, axis=0, keepdims=True))
    lb = jnp.sum(e[:layer + 1], axis=0, keepdims=True) / jnp.sum(e, axis=0, keepdims=True)
    forget = lb + (1.0 - lb) * jax.nn.sigmoid(acc)
    outs[0][...] = jnp.log(forget)
    outs[1][...] = 1.0 - forget


def _epi_residual(acc, rows, cols, outs):
    outs[0][...] = rows[0][...] + acc


def _epi_relu2(acc, rows, cols, outs):
    r = jnp.maximum(acc, 0.0)
    outs[0][...] = (r * r).astype(outs[0].dtype)


def _select_blocks_t(sc, n_valid, topk):
    nb = sc.shape[0]
    row = lax.broadcasted_iota(jnp.int32, sc.shape, 0)
    valid = row < n_valid
    sel = jnp.zeros(sc.shape, F32)
    for j in range(nb):
        sj = sc[j:j + 1, :]
        beats = ((sc > sj) | ((sc == sj) & (row < j))) & valid
        rank = jnp.sum(jnp.where(beats, 1.0, 0.0), axis=0, keepdims=True)
        chosen = jnp.where((rank < float(topk)) & (j < n_valid), 1.0, 0.0)
        sel = jnp.where(row == j, chosen, sel)
    return sel


def _moba_prompt_tile(i, q_ref, k_ref, vt_sc, o_ref, kmean, *, scale):
    blk = MOBA_BLOCK
    rows = KV_GROUP * blk
    qb = q_ref[i * blk:(i + 1) * blk, :]
    q2 = jnp.concatenate([qb[:, g * HEAD_DIM:(g + 1) * HEAD_DIM] for g in range(KV_GROUP)], axis=0)
    sel_t = _select_blocks_t(_nt_dot(kmean, q2), i, MOBA_TOPK)
    bias_t = jnp.where(sel_t > 0.5, 0.0, NEG)

    def scores(j):
        s = _nt_dot(k_ref[j * blk:(j + 1) * blk, :], q2)
        if j == i:
            kpos = lax.broadcasted_iota(jnp.int32, s.shape, 0)
            qpos = lax.broadcasted_iota(jnp.int32, s.shape, 1) % blk
            return jnp.where(kpos <= qpos, s, NEG)
        return s + bias_t[j:j + 1, :]

    blocks = list(range(i + 1))
    mrun = scores(i)
    for j in blocks[:-1]:
        mrun = jnp.maximum(mrun, scores(j))
    m = jnp.max(mrun, axis=0, keepdims=True)
    lsum = None
    acc = None
    for j in blocks:
        pe = jnp.exp2((scores(j) - m) * (scale * LOG2E))
        lsum = pe if lsum is None else lsum + pe
        d = jnp.dot(vt_sc[:, j * blk:(j + 1) * blk], pe.astype(BF16), preferred_element_type=F32)
        acc = d if acc is None else acc + d
    o = (acc / jnp.sum(lsum, axis=0, keepdims=True)).T
    for g in range(KV_GROUP):
        o_ref[i * blk:(i + 1) * blk, g * HEAD_DIM:(g + 1) * HEAD_DIM] = o[g * blk:(g + 1) * blk].astype(o_ref.dtype)


def _moba_prompt_body(q_ref, k_ref, v_ref, *rest, nb, scale, side_cast):
    if side_cast:
        w_ref, o_ref, wb_ref, kmean_sc, vt_sc = rest
        wb_ref[...] = w_ref[...].astype(BF16)
    else:
        o_ref, kmean_sc, vt_sc = rest
    p = pl.program_id(2)

    @pl.when(p == 0)
    def _():
        kall = k_ref[...].astype(F32)
        kmean_sc[...] = jnp.sum(kall.reshape(nb, MOBA_BLOCK, HEAD_DIM), axis=1) * (1.0 / MOBA_BLOCK)
        vt_sc[...] = v_ref[...].astype(F32).T.astype(BF16)

    for c in range(nb // 2):
        @pl.when(p == c)
        def _(c=c):
            kmean = kmean_sc[...].astype(BF16)
            for i in (c, nb - 1 - c):
                _moba_prompt_tile(i, q_ref, k_ref, vt_sc, o_ref, kmean, scale=scale)


def _moba_prompt(q, kb, vb, *, batch, seq, to_bf16=None):
    assert seq % (2 * MOBA_BLOCK) == 0, "query tiles are processed in (p, nb-1-p) pairs"
    nb = seq // MOBA_BLOCK
    assert nb <= LANES
    n_kv = kb.shape[1] // HEAD_DIM
    gw = KV_GROUP * HEAD_DIM
    npair = nb // 2
    steps = batch * n_kv * npair
    side_cast = to_bf16 is not None and to_bf16.shape[0] % (steps * 16) == 0
    in_specs = [pl.BlockSpec((seq, gw), lambda b, n, p: (b, n)),
                pl.BlockSpec((seq, HEAD_DIM), lambda b, n, p: (b, n)),
                pl.BlockSpec((seq, HEAD_DIM), lambda b, n, p: (b, n))]
    out_specs = [pl.BlockSpec((seq, gw), lambda b, n, p: (b, n))]
    out_shape = [jax.ShapeDtypeStruct(q.shape, BF16)]
    args = [q, kb, vb]
    if side_cast:
        slab = pl.BlockSpec((to_bf16.shape[0] // steps, to_bf16.shape[1]),
                            lambda b, n, p: ((b * n_kv + n) * npair + p, 0))
        in_specs.append(slab)
        out_specs.append(slab)
        out_shape.append(jax.ShapeDtypeStruct(to_bf16.shape, BF16))
        args.append(to_bf16)
    body = functools.partial(_moba_prompt_body, nb=nb, scale=HEAD_DIM ** -0.5, side_cast=side_cast)
    res = pl.pallas_call(
        body,
        grid=(batch, n_kv, npair),
        in_specs=in_specs,
        out_specs=out_specs,
        out_shape=out_shape,
        scratch_shapes=[pltpu.VMEM((nb, HEAD_DIM), F32),
                        pltpu.VMEM((HEAD_DIM, seq), BF16)],
        compiler_params=_params(("arbitrary", "arbitrary", "arbitrary")),
        name="moba_prompt",
    )(*args)
    return res[0], (res[1] if side_cast else None)


def _moba_sample_body(pt_ref, q_ref, kn_ref, vn_ref, *rest, pps, nch, n_kv, t_new, n_pages, scale):
    k_refs = rest[:pps]
    v_refs = rest[pps:2 * pps]
    o_ref = rest[2 * pps]
    qbd_sc, s_sc, p_sc, km_sc, oacc_sc, l_sc = rest[2 * pps + 1:]
    c = pl.program_id(1)
    ppb = MOBA_BLOCK // PAGE_SIZE
    nb = n_pages // ppb
    gt = KV_GROUP * t_new
    rows = n_kv * gt
    kvw = n_kv * HEAD_DIM

    @pl.when(c == 0)
    def _():
        qv = q_ref[...].astype(F32)
        per_g = [jnp.concatenate([qv[:, (KV_GROUP * n + g) * HEAD_DIM:(KV_GROUP * n + g + 1) * HEAD_DIM]
                                  for n in range(n_kv)], axis=1) for g in range(KV_GROUP)]
        tiled = jnp.concatenate([per_g[g] for n in range(n_kv) for g in range(KV_GROUP)], axis=0)
        rown = lax.broadcasted_iota(jnp.int32, (rows, kvw), 0) // gt
        coln = lax.broadcasted_iota(jnp.int32, (rows, kvw), 1) // HEAD_DIM
        qbd_sc[...] = jnp.where(rown == coln, tiled, 0.0).astype(BF16)

    def page_rows(ref):
        n_tok = ref.shape[0] // n_kv
        return jnp.concatenate([ref[pl.ds(n, n_tok, stride=n_kv), :] for n in range(n_kv)], axis=1)

    @pl.when(c < nch)
    def _():
        qbd = qbd_sc[...]
        colsum = None
        for i in range(pps):
            cs = jnp.sum(k_refs[i][...].reshape(PAGE_SIZE, n_kv, HEAD_DIM), axis=0)
            colsum = cs if i % ppb == 0 else colsum + cs
            if i % ppb == ppb - 1:
                km_sc[pl.ds(c * (pps // ppb) + i // ppb, 1)] = colsum[None]
            col0 = pl.multiple_of((c * pps + i) * PAGE_SIZE, PAGE_SIZE)
            s_sc[:, pl.ds(col0, PAGE_SIZE)] = _nt_dot(qbd, page_rows(k_refs[i]).astype(BF16))

    @pl.when(c == nch - 1)
    def _():
        qbd = qbd_sc[...]
        kmean = (jnp.concatenate([km_sc[:, n, :] for n in range(n_kv)], axis=1)
                 * (1.0 / MOBA_BLOCK)).astype(BF16)
        sc_t = _nt_dot(kmean, qbd)
        sel_t = _select_blocks_t(sc_t, nb, MOBA_TOPK)
        bias = jnp.where(sel_t > 0.5, 0.0, NEG).T
        zpad = jnp.zeros((LANES - t_new, kvw), F32)
        knp = jnp.concatenate([kn_ref[...].astype(F32), zpad], axis=0).astype(BF16)
        vnp = jnp.concatenate([vn_ref[...].astype(F32), zpad], axis=0).astype(BF16)
        s_own = _nt_dot(qbd, knp)
        tok = lax.broadcasted_iota(jnp.int32, s_own.shape, 0) % t_new
        key = lax.broadcasted_iota(jnp.int32, s_own.shape, 1)
        s_own = jnp.where(key <= tok, s_own, NEG)
        m = jnp.max(s_own, axis=-1, keepdims=True)
        for j in range(nb):
            sl = slice(j * MOBA_BLOCK, (j + 1) * MOBA_BLOCK)
            sb = s_sc[:, sl] + bias[:, j:j + 1]
            s_sc[:, sl] = sb
            m = jnp.maximum(m, jnp.max(sb, axis=-1, keepdims=True))
        p_own = jnp.exp((s_own - m) * scale)
        l = jnp.sum(p_own, axis=-1, keepdims=True)
        for j in range(nb):
            sl = slice(j * MOBA_BLOCK, (j + 1) * MOBA_BLOCK)
            p = jnp.exp((s_sc[:, sl] - m) * scale)
            l = l + jnp.sum(p, axis=-1, keepdims=True)
            p_sc[:, sl] = p.astype(BF16)
        l_sc[...] = jnp.broadcast_to(l, l_sc.shape)
        oacc_sc[...] = jnp.dot(p_own.astype(BF16), vnp, preferred_element_type=F32)

    @pl.when(c >= nch)
    def _():
        acc = oacc_sc[...]
        for i in range(pps):
            col0 = pl.multiple_of(((c - nch) * pps + i) * PAGE_SIZE, PAGE_SIZE)
            acc = acc + jnp.dot(p_sc[:, pl.ds(col0, PAGE_SIZE)], page_rows(v_refs[i]).astype(BF16),
                                preferred_element_type=F32)
        oacc_sc[...] = acc

    @pl.when(c == 2 * nch - 1)
    def _():
        o = oacc_sc[...] / l_sc[:, :1]
        for n in range(n_kv):
            for g in range(KV_GROUP):
                r0 = n * gt + g * t_new
                h = KV_GROUP * n + g
                o_ref[:, h * HEAD_DIM:(h + 1) * HEAD_DIM] = (
                    o[r0:r0 + t_new, n * HEAD_DIM:(n + 1) * HEAD_DIM].astype(o_ref.dtype))


def _moba_sample(q, kn, vn, cache_k, cache_v, page_table, *, batch, t_new):
    n_pages = page_table.shape[1]
    ppb = MOBA_BLOCK // PAGE_SIZE
    assert n_pages % ppb == 0 and n_pages // ppb >= MOBA_TOPK
    kvw = kn.shape[1]
    n_kv = kvw // HEAD_DIM
    pps = _tile(n_pages, 16, ppb)
    nch = n_pages // pps
    rows = n_kv * KV_GROUP * t_new
    past = n_pages * PAGE_SIZE

    def k_map(i):
        return lambda b, c, pt: (pt[b, jnp.minimum(c, nch - 1) * pps + i], 0)

    def v_map(i):
        return lambda b, c, pt: (pt[b, jnp.maximum(c - nch, 0) * pps + i], 0)

    in_specs = [pl.BlockSpec((t_new, q.shape[1]), lambda b, c, pt: (b, 0)),
                pl.BlockSpec((t_new, kvw), lambda b, c, pt: (b, 0)),
                pl.BlockSpec((t_new, kvw), lambda b, c, pt: (b, 0))]
    in_specs += [pl.BlockSpec((PAGE_SIZE * n_kv, HEAD_DIM), k_map(i)) for i in range(pps)]
    in_specs += [pl.BlockSpec((PAGE_SIZE * n_kv, HEAD_DIM), v_map(i)) for i in range(pps)]
    body = functools.partial(_moba_sample_body, pps=pps, nch=nch, n_kv=n_kv, t_new=t_new,
                             n_pages=n_pages, scale=HEAD_DIM ** -0.5)
    grid_spec = pltpu.PrefetchScalarGridSpec(
        num_scalar_prefetch=1,
        grid=(batch, 2 * nch),
        in_specs=in_specs,
        out_specs=pl.BlockSpec((t_new, q.shape[1]), lambda b, c, pt: (b, 0)),
        scratch_shapes=[pltpu.VMEM((rows, kvw), BF16),
                        pltpu.VMEM((rows, past), F32),
                        pltpu.VMEM((rows, past), BF16),
                        pltpu.VMEM((n_pages // ppb, n_kv, HEAD_DIM), F32),
                        pltpu.VMEM((rows, kvw), F32),
                        pltpu.VMEM((rows, LANES), F32)],
    )
    return pl.pallas_call(
        body,
        grid_spec=grid_spec,
        out_shape=jax.ShapeDtypeStruct(q.shape, BF16),
        compiler_params=_params(("arbitrary", "arbitrary")),
        name="moba_sample",
    )(page_table, q, kn, vn, *([cache_k] * pps), *([cache_v] * pps))


def _hgrn_body(q_ref, lf_ref, k_ref, v_ref, gate_ref, gn_ref, h0_ref, o_ref, hout_ref, ht_sc,
               *, hb, chunk, sub, n_chunks, n_tsteps):
    t = pl.program_id(2)
    ns = chunk // sub

    @pl.when(t == 0)
    def _():
        for h in range(hb):
            ht_sc[h] = h0_ref[h].T

    r_i = lax.broadcasted_iota(jnp.int32, (chunk, chunk), 0)
    c_i = lax.broadcasted_iota(jnp.int32, (chunk, chunk), 1)
    causal = c_i <= r_i
    tri = jnp.where(causal, 1.0, 0.0).astype(BF16)
    width = hb * HEAD_DIM
    rowc = lax.broadcasted_iota(jnp.int32, (chunk, width), 0)
    heads = [slice(h * HEAD_DIM, (h + 1) * HEAD_DIM) for h in range(hb)]

    def one_chunk(ci, carry):
        rs = pl.ds(pl.multiple_of(ci * chunk, chunk), chunk)
        lf = lf_ref[rs, :]
        q = q_ref[rs, :]
        k = k_ref[rs, :]
        v = v_ref[rs, :]
        hi = lf.astype(BF16)
        lo = (lf - hi.astype(F32)).astype(BF16)
        bb = jnp.dot(tri, jnp.concatenate([hi, lo], axis=1), preferred_element_type=F32)
        b = bb[:, :width] + bb[:, width:]
        b_last = b[chunk - 1:chunk, :]
        qe = (q * jnp.exp(b)).astype(BF16)
        kd = (k * jnp.exp(b_last - b)).astype(BF16)
        decay = jnp.exp(b_last)
        refs = [jnp.zeros((1, width), F32)] + [b[sub * s - 1:sub * s, :] for s in range(1, ns)]
        gfull = jnp.concatenate([jnp.broadcast_to(r, (sub, width)) for r in refs], axis=0)
        qh = q * jnp.exp(b - gfull)
        qparts = [jnp.where((rowc >= sub * s) & (rowc < sub * (s + 1)), qh, 0.0).astype(BF16) for s in range(ns)]
        kparts = [jnp.where(rowc < sub * (s + 1), k * jnp.exp(refs[s] - b), 0.0).astype(BF16) for s in range(ns)]
        hts = [ht_sc[h] for h in range(hb)]
        o_state = [_nt_dot(qe[:, cs], hts[h].astype(BF16)) for h, cs in enumerate(heads)]
        att = [_nt_dot(jnp.concatenate([p[:, cs] for p in qparts], axis=1),
                       jnp.concatenate([p[:, cs] for p in kparts], axis=1)) for cs in heads]
        upd = [_tn_dot(v[:, cs], kd[:, cs]) for cs in heads]
        o_att = [jnp.dot(jnp.where(causal, att[h], 0.0).astype(BF16), v[:, cs], preferred_element_type=F32)
                 for h, cs in enumerate(heads)]
        for h, cs in enumerate(heads):
            ht_sc[h] = hts[h] * decay[:, cs] + upd[h]
            o = o_state[h] + o_att[h]
            y = o * lax.rsqrt(jnp.mean(o * o, axis=-1, keepdims=True) + EPS) * gn_ref[:, cs]
            o_ref[rs, cs] = (y * gate_ref[rs, cs]).astype(o_ref.dtype)
        return carry

    lax.fori_loop(0, n_chunks, one_chunk, 0)

    @pl.when(t == n_tsteps - 1)
    def _():
        for h in range(hb):
            hout_ref[h] = ht_sc[h].T


def _hgrn(hq, lf, hk, hv, gate, gnorm, h0, *, batch, seq):
    width = hq.shape[1]
    n_heads = width // HEAD_DIM
    hb = _tile(n_heads, 16)
    chunk = HG_CHUNK if seq % HG_CHUNK == 0 else seq
    sub = HG_SUB if chunk % HG_SUB == 0 else chunk
    tc = _tile(seq, 256, chunk)
    nt = seq // tc
    bw = hb * HEAD_DIM
    row_spec = pl.BlockSpec((tc, bw), lambda b, g, t: (b * nt + t, g))
    st_spec = pl.BlockSpec((None, hb, HEAD_DIM, HEAD_DIM), lambda b, g, t: (b, g, 0, 0))
    body = functools.partial(_hgrn_body, hb=hb, chunk=chunk, sub=sub, n_chunks=tc // chunk, n_tsteps=nt)
    return pl.pallas_call(
        body,
        grid=(batch, n_heads // hb, nt),
        in_specs=[row_spec, row_spec, row_spec, row_spec, row_spec,
                  pl.BlockSpec((1, bw), lambda b, g, t: (0, g)), st_spec],
        out_specs=[row_spec, st_spec],
        out_shape=[jax.ShapeDtypeStruct(hq.shape, BF16), jax.ShapeDtypeStruct(h0.shape, F32)],
        scratch_shapes=[pltpu.VMEM((hb, HEAD_DIM, HEAD_DIM), F32)],
        compiler_params=_params(("arbitrary", "arbitrary", "arbitrary")),
        name="hgrn",
    )(hq, lf, hk, hv, gate, gnorm, h0)


_IN_NAMES = ("q", "k", "kb", "v", "vb", "hq", "lf", "hk", "hv", "gate")


def _layer_inputs(xp, xs, w, *, tm):
    d = xp.shape[1]
    attn_w = d // 2
    kv_w = attn_w // KV_GROUP
    hg_w = d - attn_w
    xn_p = _rmsnorm(xp, w["norm1_g"], _tile(xp.shape[0], 256, 8))
    xn_s = _rmsnorm(xs, w["norm1_g"], _tile(xs.shape[0], 256, 8))
    starts = [0, attn_w, attn_w + kv_w, attn_w + 2 * kv_w, attn_w + 2 * kv_w + hg_w,
              attn_w + 2 * kv_w + 2 * hg_w, attn_w + 2 * kv_w + 3 * hg_w]

    def proj(name, seg, n, epilogue, cols=(), outs=(), whole=False):
        tn = n if whole else _tile(n, 1024 if len(outs) == 1 else 512, LANES)
        while starts[seg] % tn:
            tn = _tile(n, tn - 1, LANES)
        assert tn == n or not whole
        tm_call = _tile(xn_p.shape[0], min(tm, KV_TM), 8) if whole else tm
        return _matmul(name, [(xn_p, w["w_in"], 0)], n=n, col_off=starts[seg], tm=tm_call, tn=tn, nk=1,
                       epilogue=epilogue, cols=cols, outs=outs, rider=([xn_s], []))

    res = []
    res += proj("proj_q", 0, attn_w, _epi_head_norm, cols=(w["q_gain"],), outs=(BF16,))
    res += proj("proj_k", 1, kv_w, _epi_head_norm, cols=(w["k_gain"],), outs=((F32, "heads"), BF16), whole=True)
    res += proj("proj_v", 2, kv_w, _epi_copy, outs=((F32, "heads"), BF16), whole=True)
    res += proj("proj_hq", 3, hg_w, _epi_silu, outs=(F32,))
    res += proj("proj_hf", 4, hg_w, functools.partial(_epi_forget, layer=0), cols=(w["lb_logits"],),
                outs=(F32, F32))
    res += proj("proj_hi", 5, hg_w, _epi_copy, outs=(BF16,))
    res += proj("proj_hg", 6, hg_w, _epi_sigmoid, outs=(F32,))
    counts = (1, 2, 2, 1, 2, 1, 1)
    prompt, sample, at = [], [], 0
    for c in counts:
        prompt += res[at:at + c]
        sample += res[at + c:at + 2 * c]
        at += 2 * c
    return dict(zip(_IN_NAMES, prompt)), dict(zip(_IN_NAMES, sample))


def _layer_outputs(xp, xs, attn_p, attn_s, hg_p, hg_s, w, *, tm):
    d = xp.shape[1]
    d_ff = w["w_up"].shape[1]
    aw = attn_p.shape[1]
    x1_p, x1_s = _matmul("proj_out", [(attn_p, w["w_out"], 0), (hg_p, w["w_out"], aw)], n=d, col_off=0, tm=tm,
                         tn=_tile(d, 512, LANES), nk=1, epilogue=_epi_residual, rows=(xp,), outs=(F32,),
                         rider=([attn_s, hg_s], [xs]))
    hn_p = _rmsnorm(x1_p, w["norm2_g"], _tile(xp.shape[0], 256, 8))
    hn_s = _rmsnorm(x1_s, w["norm2_g"], _tile(xs.shape[0], 256, 8))
    act_p, act_s = _matmul("mlp_up", [(hn_p, w["w_up"], 0)], n=d_ff, col_off=0, tm=tm, tn=_tile(d_ff, 512, LANES),
                           nk=1, epilogue=_epi_relu2, outs=(BF16,), rider=([hn_s], []))
    nk = d_ff // _tile(d_ff, MLP_DOWN_TK, LANES)
    y_p, y_s = _matmul("mlp_down", [(act_p, w["w_down"], 0)], n=d, col_off=0, tm=tm, tn=_tile(d, 1024, LANES),
                       nk=nk, epilogue=_epi_residual, rows=(x1_p,), outs=(F32,), rider=([act_s], [x1_s]),
                       acc_in_out=True)
    return y_p, y_s


def kernel(x_prompt, x_sample, cache_k, cache_v, state_h, page_table, norm1_g, w_in, q_norm_g, k_norm_g,
           lb_logits, hg_norm_g, w_out, norm2_g, w_up, w_down):
    depth = w_in.shape[0]
    assert depth == 1, "one layer"
    bp, tp, d = x_prompt.shape
    bs, ts, _ = x_sample.shape
    attn_w = d // 2
    n_q = attn_w // HEAD_DIM
    n_kv = n_q // KV_GROUP
    kv_w = n_kv * HEAD_DIM
    hg_w = d - attn_w
    n_hg = hg_w // HEAD_DIM

    w = dict(
        norm1_g=norm1_g[0], norm2_g=norm2_g[0],
        w_in=w_in[0].astype(BF16), w_out=w_out[0], w_up=w_up[0],
        q_gain=jnp.tile(q_norm_g[0], n_q).reshape(1, attn_w),
        k_gain=jnp.tile(k_norm_g[0], n_kv).reshape(1, kv_w),
        lb_logits=lb_logits,
    )
    gnorm = hg_norm_g[0].reshape(1, hg_w)

    xp = x_prompt.reshape(bp * tp, d)
    xs = x_sample.reshape(bs * ts, d)
    tm = _tile(bp * tp, 1024, 8)
    gp, gs = _layer_inputs(xp, xs, w, tm=tm)

    attn_p, w_down_b = _moba_prompt(gp["q"], gp["kb"], gp["vb"], batch=bp, seq=tp, to_bf16=w_down[0])
    w["w_down"] = w_down[0].astype(BF16) if w_down_b is None else w_down_b
    h0_p = jnp.zeros((bp, n_hg, HEAD_DIM, HEAD_DIM), F32)
    hg_p, h_p = _hgrn(gp["hq"], gp["lf"], gp["hk"], gp["hv"], gp["gate"], gnorm, h0_p, batch=bp, seq=tp)
    attn_s = _moba_sample(gs["q"], gs["kb"], gs["vb"], cache_k[0].reshape(-1, HEAD_DIM),
                          cache_v[0].reshape(-1, HEAD_DIM), page_table, batch=bs, t_new=ts)
    hg_s, h_s = _hgrn(gs["hq"], gs["lf"], gs["hk"], gs["hv"], gs["gate"], gnorm, state_h[0], batch=bs, seq=ts)

    y_p, y_s = _layer_outputs(xp, xs, attn_p, attn_s, hg_p, hg_s, w, tm=tm)

    return (y_p.reshape(bp, tp, d), y_s.reshape(bs, ts, d),
            gp["k"].reshape(1, bp, tp, n_kv, HEAD_DIM), gp["v"].reshape(1, bp, tp, n_kv, HEAD_DIM),
            h_p.reshape(1, bp, n_hg, HEAD_DIM, HEAD_DIM),
            gs["k"].reshape(1, bs, ts, n_kv, HEAD_DIM), gs["v"].reshape(1, bs, ts, n_kv, HEAD_DIM),
            h_s.reshape(1, bs, n_hg, HEAD_DIM, HEAD_DIM))
```

```python
import functools

import jax
import jax.numpy as jnp
from jax import lax
from jax.experimental import pallas as pl
from jax.experimental.pallas import tpu as pltpu

F32 = jnp.float32
BF16 = jnp.bfloat16

HEAD_DIM = 128
KV_GROUP = 2
PAGE_SIZE = 128
MOBA_BLOCK = 256
MOBA_TOPK = 3
HG_CHUNK = 64
HG_SUB = 16
EPS = 1e-6
NEG = -1e30
LOG2E = 1.4426950408889634
LANES = 128
SUB_N = 512
MLP_DOWN_TK = 2048
KV_TM = 1024
VMEM_LIMIT = 60 * 1024 * 1024


def _tile(n, target, mult=1):
    for t in range(min(n, target), 0, -1):
        if n % t == 0 and t % mult == 0:
            return t
    return n


def _params(sem):
    return pltpu.CompilerParams(dimension_semantics=sem, vmem_limit_bytes=VMEM_LIMIT)


def _nt_dot(a, b):
    return lax.dot_general(a, b, (((1,), (1,)), ((), ())), preferred_element_type=F32)


def _tn_dot(a, b):
    return lax.dot_general(a, b, (((0,), (0,)), ((), ())), preferred_element_type=F32)


def _rmsnorm_body(x_ref, g_ref, o_ref):
    x = x_ref[...]
    ms = jnp.mean(x * x, axis=-1, keepdims=True)
    o_ref[...] = (x * lax.rsqrt(ms + EPS) * g_ref[...]).astype(o_ref.dtype)


def _rmsnorm(x, g, tm):
    m, d = x.shape
    return pl.pallas_call(
        _rmsnorm_body,
        grid=(m // tm,),
        in_specs=[pl.BlockSpec((tm, d), lambda i: (i, 0)), pl.BlockSpec((1, d), lambda i: (0, 0))],
        out_specs=pl.BlockSpec((tm, d), lambda i: (i, 0)),
        out_shape=jax.ShapeDtypeStruct((m, d), BF16),
        compiler_params=_params(("arbitrary",)),
        name="rmsnorm",
    )(x, g.reshape(1, d))


def _col_view(ref, c0, width):
    if len(ref.shape) == 3:
        return ref.at[:, c0 // HEAD_DIM:(c0 + width) // HEAD_DIM, :]
    return ref.at[:, c0:c0 + width]


def _matmul_tile(xs, ws, row_refs, col_refs, out_refs, acc_ref, *, nk, epilogue):
    tn = ws[0].shape[1]
    if nk == 1:
        sub = SUB_N if tn % SUB_N == 0 else tn
        for c0 in range(0, tn, sub):
            part = None
            for x_ref, w_ref in zip(xs, ws):
                d = jnp.dot(x_ref[...], w_ref[:, c0:c0 + sub].astype(BF16), preferred_element_type=F32)
                part = d if part is None else part + d
            epilogue(part, [_col_view(r, c0, sub) for r in row_refs], [_col_view(r, c0, sub) for r in col_refs],
                     [_col_view(r, c0, sub) for r in out_refs])
        return
    part = None
    for x_ref, w_ref in zip(xs, ws):
        d = jnp.dot(x_ref[...], w_ref[...].astype(BF16), preferred_element_type=F32)
        part = d if part is None else part + d
    k = pl.program_id(2)

    @pl.when(k == 0)
    def _():
        acc_ref[...] = part

    @pl.when(k > 0)
    def _():
        acc_ref[...] += part

    @pl.when(k == nk - 1)
    def _():
        epilogue(acc_ref[...], row_refs, col_refs, out_refs)


def _matmul_body(*refs, n_pairs, n_rows, n_cols, n_outs, nk, epilogue, rider, acc_in_out):
    it = iter(refs)
    take = lambda cnt: [next(it) for _ in range(cnt)]
    xw = take(2 * n_pairs)
    xs, ws = xw[0::2], xw[1::2]
    row_refs, col_refs = take(n_rows), take(n_cols)
    xs2, row_refs2 = (take(n_pairs), take(n_rows)) if rider else ([], [])
    out_refs = take(n_outs)
    out_refs2 = take(n_outs) if rider else []
    acc_ref = out_refs[0] if acc_in_out else (next(it) if nk > 1 else None)
    acc_ref2 = next(it) if rider and nk > 1 else None
    _matmul_tile(xs, ws, row_refs, col_refs, out_refs, acc_ref, nk=nk, epilogue=epilogue)
    if rider:
        @pl.when(pl.program_id(0) == 0)
        def _():
            _matmul_tile(xs2, ws, row_refs2, col_refs, out_refs2, acc_ref2, nk=nk, epilogue=epilogue)


def _matmul(name, pairs, *, n, col_off, tm, tn, nk, epilogue, rows=(), cols=(), outs=(), rider=None,
            acc_in_out=False):
    m = pairs[0][0].shape[0]
    assert m % tm == 0 and n % tn == 0 and col_off % tn == 0
    joff = col_off // tn
    nj = n // tn
    first_j = lambda i, j: jnp.where(i == 0, j, nj - 1)
    first_k = lambda i, k: jnp.where(i == 0, k, nk - 1)
    in_specs, args = [], []
    for x, w, row_off in pairs:
        kdim = x.shape[1]
        assert kdim % nk == 0
        tk = kdim // nk
        assert row_off % tk == 0
        in_specs += [pl.BlockSpec((tm, tk), lambda i, j, k: (i, k)),
                     pl.BlockSpec((tk, tn), lambda i, j, k, koff=row_off // tk: (k + koff, j + joff))]
        args += [x, w]
    for r in rows:
        mode = dict(pipeline_mode=pl.Buffered(1)) if nk > 1 else {}
        in_specs.append(pl.BlockSpec((tm, tn), lambda i, j, k: (i, j), **mode))
        args.append(r)
    for c in cols:
        in_specs.append(pl.BlockSpec((c.shape[0], tn), lambda i, j, k: (0, j)))
        args.append(c)
    m2 = 0
    if rider:
        xs2, rows2 = rider
        m2 = xs2[0].shape[0]
        for x2, (x, _, _) in zip(xs2, pairs):
            assert x2.shape[1] == x.shape[1]
            in_specs.append(pl.BlockSpec((m2, x.shape[1] // nk), lambda i, j, k: (0, first_k(i, k))))
            args.append(x2)
        for r in rows2:
            in_specs.append(pl.BlockSpec((m2, tn), lambda i, j, k: (0, first_j(i, j))))
            args.append(r)
    out_specs, out_shape = [], []

    def add_outs(mrows, trows, imap, jmap):
        for o in outs:
            if isinstance(o, tuple):
                out_specs.append(pl.BlockSpec((trows, tn // HEAD_DIM, HEAD_DIM),
                                              lambda i, j, k: (imap(i), jmap(i, j), 0)))
                out_shape.append(jax.ShapeDtypeStruct((mrows, n // HEAD_DIM, HEAD_DIM), o[0]))
            else:
                out_specs.append(pl.BlockSpec((trows, tn), lambda i, j, k: (imap(i), jmap(i, j))))
                out_shape.append(jax.ShapeDtypeStruct((mrows, n), o))

    add_outs(m, tm, lambda i: i, lambda i, j: j)
    if rider:
        add_outs(m2, m2, lambda i: 0, first_j)
    scratch = []
    if nk > 1:
        assert not acc_in_out or (len(outs) == 1 and outs[0] == F32)
        scratch = ([] if acc_in_out else [pltpu.VMEM((tm, tn), F32)]) + ([pltpu.VMEM((m2, tn), F32)] if rider else [])
    body = functools.partial(_matmul_body, n_pairs=len(pairs), n_rows=len(rows), n_cols=len(cols),
                             n_outs=len(outs), nk=nk, epilogue=epilogue, rider=bool(rider),
                             acc_in_out=acc_in_out and nk > 1)
    return pl.pallas_call(
        body,
        grid=(m // tm, nj, nk),
        in_specs=in_specs,
        out_specs=out_specs,
        out_shape=out_shape,
        scratch_shapes=scratch,
        compiler_params=_params(("arbitrary", "arbitrary", "arbitrary")),
        name=name,
    )(*args)


def _store_head(o, h, y):
    if len(o.shape) == 3:
        o[:, h, :] = y.astype(o.dtype)
    else:
        o[:, h * HEAD_DIM:(h + 1) * HEAD_DIM] = y.astype(o.dtype)


def _epi_head_norm(acc, rows, cols, outs):
    g = cols[0][...]
    for h in range(acc.shape[1] // HEAD_DIM):
        sl = slice(h * HEAD_DIM, (h + 1) * HEAD_DIM)
        blk = acc[:, sl]
        y = blk * lax.rsqrt(jnp.mean(blk * blk, axis=-1, keepdims=True) + EPS) * g[:, sl]
        for o in outs:
            _store_head(o, h, y)


def _epi_copy(acc, rows, cols, outs):
    for h in range(acc.shape[1] // HEAD_DIM):
        for o in outs:
            _store_head(o, h, acc[:, h * HEAD_DIM:(h + 1) * HEAD_DIM])


def _epi_silu(acc, rows, cols, outs):
    outs[0][...] = (acc * jax.nn.sigmoid(acc)).astype(outs[0].dtype)


def _epi_sigmoid(acc, rows, cols, outs):
    outs[0][...] = jax.nn.sigmoid(acc).astype(outs[0].dtype)


def _epi_forget(acc, rows, cols, outs, *, layer):
    logits = cols[0][...]
    e = jnp.exp(logits - jnp.max(logits, axis=0, keepdims=True))
    lb = jnp.sum(e[:layer + 1], axis=0, keepdims=True) / jnp.sum(e, axis=0, keepdims=True)
    forget = lb + (1.0 - lb) * jax.nn.sigmoid(acc)
    outs[0][...] = jnp.log(forget)
    outs[1][...] = 1.0 - forget


def _epi_residual(acc, rows, cols, outs):
    outs[0][...] = rows[0][...] + acc


def _epi_relu2(acc, rows, cols, outs):
    r = jnp.maximum(acc, 0.0)
    outs[0][...] = (r * r).astype(outs[0].dtype)


def _select_blocks_t(sc, n_valid, topk):
    nb = sc.shape[0]
    row = lax.broadcasted_iota(jnp.int32, sc.shape, 0)
    valid = row < n_valid
    sel = jnp.zeros(sc.shape, F32)
    for j in range(nb):
        sj = sc[j:j + 1, :]
        beats = ((sc > sj) | ((sc == sj) & (row < j))) & valid
        rank = jnp.sum(jnp.where(beats, 1.0, 0.0), axis=0, keepdims=True)
        chosen = jnp.where((rank < float(topk)) & (j < n_valid), 1.0, 0.0)
        sel = jnp.where(row == j, chosen, sel)
    return sel


def _moba_prompt_tile(i, q_ref, k_ref, vt_sc, o_ref, kmean, *, scale):
    blk = MOBA_BLOCK
    rows = KV_GROUP * blk
    qb = q_ref[i * blk:(i + 1) * blk, :]
    q2 = jnp.concatenate([qb[:, g * HEAD_DIM:(g + 1) * HEAD_DIM] for g in range(KV_GROUP)], axis=0)
    sel_t = _select_blocks_t(_nt_dot(kmean, q2), i, MOBA_TOPK)
    bias_t = jnp.where(sel_t > 0.5, 0.0, NEG)

    def scores(j):
        s = _nt_dot(k_ref[j * blk:(j + 1) * blk, :], q2)
        if j == i:
            kpos = lax.broadcasted_iota(jnp.int32, s.shape, 0)
            qpos = lax.broadcasted_iota(jnp.int32, s.shape, 1) % blk
            return jnp.where(kpos <= qpos, s, NEG)
        return s + bias_t[j:j + 1, :]

    blocks = list(range(i + 1))
    mrun = scores(i)
    for j in blocks[:-1]:
        mrun = jnp.maximum(mrun, scores(j))
    m = jnp.max(mrun, axis=0, keepdims=True)
    lsum = None
    acc = None
    for j in blocks:
        pe = jnp.exp2((scores(j) - m) * (scale * LOG2E))
        lsum = pe if lsum is None else lsum + pe
        d = jnp.dot(vt_sc[:, j * blk:(j + 1) * blk], pe.astype(BF16), preferred_element_type=F32)
        acc = d if acc is None else acc + d
    o = (acc / jnp.sum(lsum, axis=0, keepdims=True)).T
    for g in range(KV_GROUP):
        o_ref[i * blk:(i + 1) * blk, g * HEAD_DIM:(g + 1) * HEAD_DIM] = o[g * blk:(g + 1) * blk].astype(o_ref.dtype)


def _moba_prompt_body(q_ref, k_ref, v_ref, *rest, nb, scale, side_cast):
    if side_cast:
        w_ref, o_ref, wb_ref, kmean_sc, vt_sc = rest
        wb_ref[...] = w_ref[...].astype(BF16)
    else:
        o_ref, kmean_sc, vt_sc = rest
    p = pl.program_id(2)

    @pl.when(p == 0)
    def _():
        kall = k_ref[...].astype(F32)
        kmean_sc[...] = jnp.sum(kall.reshape(nb, MOBA_BLOCK, HEAD_DIM), axis=1) * (1.0 / MOBA_BLOCK)
        vt_sc[...] = v_ref[...].astype(F32).T.astype(BF16)

    for c in range(nb // 2):
        @pl.when(p == c)
        def _(c=c):
            kmean = kmean_sc[...].astype(BF16)
            for i in (c, nb - 1 - c):
                _moba_prompt_tile(i, q_ref, k_ref, vt_sc, o_ref, kmean, scale=scale)


def _moba_prompt(q, kb, vb, *, batch, seq, to_bf16=None):
    assert seq % (2 * MOBA_BLOCK) == 0, "query tiles are processed in (p, nb-1-p) pairs"
    nb = seq // MOBA_BLOCK
    assert nb <= LANES
    n_kv = kb.shape[1] // HEAD_DIM
    gw = KV_GROUP * HEAD_DIM
    npair = nb // 2
    steps = batch * n_kv * npair
    side_cast = to_bf16 is not None and to_bf16.shape[0] % (steps * 16) == 0
    in_specs = [pl.BlockSpec((seq, gw), lambda b, n, p: (b, n)),
                pl.BlockSpec((seq, HEAD_DIM), lambda b, n, p: (b, n)),
                pl.BlockSpec((seq, HEAD_DIM), lambda b, n, p: (b, n))]
    out_specs = [pl.BlockSpec((seq, gw), lambda b, n, p: (b, n))]
    out_shape = [jax.ShapeDtypeStruct(q.shape, BF16)]
    args = [q, kb, vb]
    if side_cast:
        slab = pl.BlockSpec((to_bf16.shape[0] // steps, to_bf16.shape[1]),
                            lambda b, n, p: ((b * n_kv + n) * npair + p, 0))
        in_specs.append(slab)
        out_specs.append(slab)
        out_shape.append(jax.ShapeDtypeStruct(to_bf16.shape, BF16))
        args.append(to_bf16)
    body = functools.partial(_moba_prompt_body, nb=nb, scale=HEAD_DIM ** -0.5, side_cast=side_cast)
    res = pl.pallas_call(
        body,
        grid=(batch, n_kv, npair),
        in_specs=in_specs,
        out_specs=out_specs,
        out_shape=out_shape,
        scratch_shapes=[pltpu.VMEM((nb, HEAD_DIM), F32),
                        pltpu.VMEM((HEAD_DIM, seq), BF16)],
        compiler_params=_params(("arbitrary", "arbitrary", "arbitrary")),
        name="moba_prompt",
    )(*args)
    return res[0], (res[1] if side_cast else None)


def _moba_sample_body(pt_ref, q_ref, kn_ref, vn_ref, *rest, pps, nch, n_kv, t_new, n_pages, scale):
    k_refs = rest[:pps]
    v_refs = rest[pps:2 * pps]
    o_ref = rest[2 * pps]
    qbd_sc, s_sc, p_sc, km_sc, oacc_sc, l_sc = rest[2 * pps + 1:]
    c = pl.program_id(1)
    ppb = MOBA_BLOCK // PAGE_SIZE
    nb = n_pages // ppb
    gt = KV_GROUP * t_new
    rows = n_kv * gt
    kvw = n_kv * HEAD_DIM

    @pl.when(c == 0)
    def _():
        qv = q_ref[...].astype(F32)
        per_g = [jnp.concatenate([qv[:, (KV_GROUP * n + g) * HEAD_DIM:(KV_GROUP * n + g + 1) * HEAD_DIM]
                                  for n in range(n_kv)], axis=1) for g in range(KV_GROUP)]
        tiled = jnp.concatenate([per_g[g] for n in range(n_kv) for g in range(KV_GROUP)], axis=0)
        rown = lax.broadcasted_iota(jnp.int32, (rows, kvw), 0) // gt
        coln = lax.broadcasted_iota(jnp.int32, (rows, kvw), 1) // HEAD_DIM
        qbd_sc[...] = jnp.where(rown == coln, tiled, 0.0).astype(BF16)

    def page_rows(ref):
        n_tok = ref.shape[0] // n_kv
        return jnp.concatenate([ref[pl.ds(n, n_tok, stride=n_kv), :] for n in range(n_kv)], axis=1)

    @pl.when(c < nch)
    def _():
        qbd = qbd_sc[...]
        colsum = None
        for i in range(pps):
            cs = jnp.sum(k_refs[i][...].reshape(PAGE_SIZE, n_kv, HEAD_DIM), axis=0)
            colsum = cs if i % ppb == 0 else colsum + cs
            if i % ppb == ppb - 1:
                km_sc[pl.ds(c * (pps // ppb) + i // ppb, 1)] = colsum[None]
            col0 = pl.multiple_of((c * pps + i) * PAGE_SIZE, PAGE_SIZE)
            s_sc[:, pl.ds(col0, PAGE_SIZE)] = _nt_dot(qbd, page_rows(k_refs[i]).astype(BF16))

    @pl.when(c == nch - 1)
    def _():
        qbd = qbd_sc[...]
        kmean = (jnp.concatenate([km_sc[:, n, :] for n in range(n_kv)], axis=1)
                 * (1.0 / MOBA_BLOCK)).astype(BF16)
        sc_t = _nt_dot(kmean, qbd)
        sel_t = _select_blocks_t(sc_t, nb, MOBA_TOPK)
        bias = jnp.where(sel_t > 0.5, 0.0, NEG).T
        zpad = jnp.zeros((LANES - t_new, kvw), F32)
        knp = jnp.concatenate([kn_ref[...].astype(F32), zpad], axis=0).astype(BF16)
        vnp = jnp.concatenate([vn_ref[...].astype(F32), zpad], axis=0).astype(BF16)
        s_own = _nt_dot(qbd, knp)
        tok = lax.broadcasted_iota(jnp.int32, s_own.shape, 0) % t_new
        key = lax.broadcasted_iota(jnp.int32, s_own.shape, 1)
        s_own = jnp.where(key <= tok, s_own, NEG)
        m = jnp.max(s_own, axis=-1, keepdims=True)
        for j in range(nb):
            sl = slice(j * MOBA_BLOCK, (j + 1) * MOBA_BLOCK)
            sb = s_sc[:, sl] + bias[:, j:j + 1]
            s_sc[:, sl] = sb
            m = jnp.maximum(m, jnp.max(sb, axis=-1, keepdims=True))
        p_own = jnp.exp((s_own - m) * scale)
        l = jnp.sum(p_own, axis=-1, keepdims=True)
        for j in range(nb):
            sl = slice(j * MOBA_BLOCK, (j + 1) * MOBA_BLOCK)
            p = jnp.exp((s_sc[:, sl] - m) * scale)
            l = l + jnp.sum(p, axis=-1, keepdims=True)
            p_sc[:, sl] = p.astype(BF16)
        l_sc[...] = jnp.broadcast_to(l, l_sc.shape)
        oacc_sc[...] = jnp.dot(p_own.astype(BF16), vnp, preferred_element_type=F32)

    @pl.when(c >= nch)
    def _():
        acc = oacc_sc[...]
        for i in range(pps):
            col0 = pl.multiple_of(((c - nch) * pps + i) * PAGE_SIZE, PAGE_SIZE)
            acc = acc + jnp.dot(p_sc[:, pl.ds(col0, PAGE_SIZE)], page_rows(v_refs[i]).astype(BF16),
                                preferred_element_type=F32)
        oacc_sc[...] = acc

    @pl.when(c == 2 * nch - 1)
    def _():
        o = oacc_sc[...] / l_sc[:, :1]
        for n in range(n_kv):
            for g in range(KV_GROUP):
                r0 = n * gt + g * t_new
                h = KV_GROUP * n + g
                o_ref[:, h * HEAD_DIM:(h + 1) * HEAD_DIM] = (
                    o[r0:r0 + t_new, n * HEAD_DIM:(n + 1) * HEAD_DIM].astype(o_ref.dtype))


def _moba_sample(q, kn, vn, cache_k, cache_v, page_table, *, batch, t_new):
    n_pages = page_table.shape[1]
    ppb = MOBA_BLOCK // PAGE_SIZE
    assert n_pages % ppb == 0 and n_pages // ppb >= MOBA_TOPK
    kvw = kn.shape[1]
    n_kv = kvw // HEAD_DIM
    pps = _tile(n_pages, 16, ppb)
    nch = n_pages // pps
    rows = n_kv * KV_GROUP * t_new
    past = n_pages * PAGE_SIZE

    def k_map(i):
        return lambda b, c, pt: (pt[b, jnp.minimum(c, nch - 1) * pps + i], 0)

    def v_map(i):
        return lambda b, c, pt: (pt[b, jnp.maximum(c - nch, 0) * pps + i], 0)

    in_specs = [pl.BlockSpec((t_new, q.shape[1]), lambda b, c, pt: (b, 0)),
                pl.BlockSpec((t_new, kvw), lambda b, c, pt: (b, 0)),
                pl.BlockSpec((t_new, kvw), lambda b, c, pt: (b, 0))]
    in_specs += [pl.BlockSpec((PAGE_SIZE * n_kv, HEAD_DIM), k_map(i)) for i in range(pps)]
    in_specs += [pl.BlockSpec((PAGE_SIZE * n_kv, HEAD_DIM), v_map(i)) for i in range(pps)]
    body = functools.partial(_moba_sample_body, pps=pps, nch=nch, n_kv=n_kv, t_new=t_new,
                             n_pages=n_pages, scale=HEAD_DIM ** -0.5)
    grid_spec = pltpu.PrefetchScalarGridSpec(
        num_scalar_prefetch=1,
        grid=(batch, 2 * nch),
        in_specs=in_specs,
        out_specs=pl.BlockSpec((t_new, q.shape[1]), lambda b, c, pt: (b, 0)),
        scratch_shapes=[pltpu.VMEM((rows, kvw), BF16),
                        pltpu.VMEM((rows, past), F32),
                        pltpu.VMEM((rows, past), BF16),
                        pltpu.VMEM((n_pages // ppb, n_kv, HEAD_DIM), F32),
                        pltpu.VMEM((rows, kvw), F32),
                        pltpu.VMEM((rows, LANES), F32)],
    )
    return pl.pallas_call(
        body,
        grid_spec=grid_spec,
        out_shape=jax.ShapeDtypeStruct(q.shape, BF16),
        compiler_params=_params(("arbitrary", "arbitrary")),
        name="moba_sample",
    )(page_table, q, kn, vn, *([cache_k] * pps), *([cache_v] * pps))


def _hgrn_body(q_ref, lf_ref, k_ref, v_ref, gate_ref, gn_ref, h0_ref, o_ref, hout_ref, ht_sc,
               *, hb, chunk, sub, n_chunks, n_tsteps):
    t = pl.program_id(2)
    ns = chunk // sub

    @pl.when(t == 0)
    def _():
        for h in range(hb):
            ht_sc[h] = h0_ref[h].T

    r_i = lax.broadcasted_iota(jnp.int32, (chunk, chunk), 0)
    c_i = lax.broadcasted_iota(jnp.int32, (chunk, chunk), 1)
    causal = c_i <= r_i
    tri = jnp.where(causal, 1.0, 0.0).astype(BF16)
    width = hb * HEAD_DIM
    rowc = lax.broadcasted_iota(jnp.int32, (chunk, width), 0)
    heads = [slice(h * HEAD_DIM, (h + 1) * HEAD_DIM) for h in range(hb)]

    def one_chunk(ci, carry):
        rs = pl.ds(pl.multiple_of(ci * chunk, chunk), chunk)
        lf = lf_ref[rs, :]
        q = q_ref[rs, :]
        k = k_ref[rs, :]
        v = v_ref[rs, :]
        hi = lf.astype(BF16)
        lo = (lf - hi.astype(F32)).astype(BF16)
        bb = jnp.dot(tri, jnp.concatenate([hi, lo], axis=1), preferred_element_type=F32)
        b = bb[:, :width] + bb[:, width:]
        b_last = b[chunk - 1:chunk, :]
        qe = (q * jnp.exp(b)).astype(BF16)
        kd = (k * jnp.exp(b_last - b)).astype(BF16)
        decay = jnp.exp(b_last)
        refs = [jnp.zeros((1, width), F32)] + [b[sub * s - 1:sub * s, :] for s in range(1, ns)]
        gfull = jnp.concatenate([jnp.broadcast_to(r, (sub, width)) for r in refs], axis=0)
        qh = q * jnp.exp(b - gfull)
        qparts = [jnp.where((rowc >= sub * s) & (rowc < sub * (s + 1)), qh, 0.0).astype(BF16) for s in range(ns)]
        kparts = [jnp.where(rowc < sub * (s + 1), k * jnp.exp(refs[s] - b), 0.0).astype(BF16) for s in range(ns)]
        hts = [ht_sc[h] for h in range(hb)]
        o_state = [_nt_dot(qe[:, cs], hts[h].astype(BF16)) for h, cs in enumerate(heads)]
        att = [_nt_dot(jnp.concatenate([p[:, cs] for p in qparts], axis=1),
                       jnp.concatenate([p[:, cs] for p in kparts], axis=1)) for cs in heads]
        upd = [_tn_dot(v[:, cs], kd[:, cs]) for cs in heads]
        o_att = [jnp.dot(jnp.where(causal, att[h], 0.0).astype(BF16), v[:, cs], preferred_element_type=F32)
                 for h, cs in enumerate(heads)]
        for h, cs in enumerate(heads):
            ht_sc[h] = hts[h] * decay[:, cs] + upd[h]
            o = o_state[h] + o_att[h]
            y = o * lax.rsqrt(jnp.mean(o * o, axis=-1, keepdims=True) + EPS) * gn_ref[:, cs]
            o_ref[rs, cs] = (y * gate_ref[rs, cs]).astype(o_ref.dtype)
        return carry

    lax.fori_loop(0, n_chunks, one_chunk, 0)

    @pl.when(t == n_tsteps - 1)
    def _():
        for h in range(hb):
            hout_ref[h] = ht_sc[h].T


def _hgrn(hq, lf, hk, hv, gate, gnorm, h0, *, batch, seq):
    width = hq.shape[1]
    n_heads = width // HEAD_DIM
    hb = _tile(n_heads, 16)
    chunk = HG_CHUNK if seq % HG_CHUNK == 0 else seq
    sub = HG_SUB if chunk % HG_SUB == 0 else chunk
    tc = _tile(seq, 256, chunk)
    nt = seq // tc
    bw = hb * HEAD_DIM
    row_spec = pl.BlockSpec((tc, bw), lambda b, g, t: (b * nt + t, g))
    st_spec = pl.BlockSpec((None, hb, HEAD_DIM, HEAD_DIM), lambda b, g, t: (b, g, 0, 0))
    body = functools.partial(_hgrn_body, hb=hb, chunk=chunk, sub=sub, n_chunks=tc // chunk, n_tsteps=nt)
    return pl.pallas_call(
        body,
        grid=(batch, n_heads // hb, nt),
        in_specs=[row_spec, row_spec, row_spec, row_spec, row_spec,
                  pl.BlockSpec((1, bw), lambda b, g, t: (0, g)), st_spec],
        out_specs=[row_spec, st_spec],
        out_shape=[jax.ShapeDtypeStruct(hq.shape, BF16), jax.ShapeDtypeStruct(h0.shape, F32)],
        scratch_shapes=[pltpu.VMEM((hb, HEAD_DIM, HEAD_DIM), F32)],
        compiler_params=_params(("arbitrary", "arbitrary", "arbitrary")),
        name="hgrn",
    )(hq, lf, hk, hv, gate, gnorm, h0)


_IN_NAMES = ("q", "k", "kb", "v", "vb", "hq", "lf", "hk", "hv", "gate")


def _layer_inputs(xp, xs, w, *, tm):
    d = xp.shape[1]
    attn_w = d // 2
    kv_w = attn_w // KV_GROUP
    hg_w = d - attn_w
    xn_p = _rmsnorm(xp, w["norm1_g"], _tile(xp.shape[0], 256, 8))
    xn_s = _rmsnorm(xs, w["norm1_g"], _tile(xs.shape[0], 256, 8))
    starts = [0, attn_w, attn_w + kv_w, attn_w + 2 * kv_w, attn_w + 2 * kv_w + hg_w,
              attn_w + 2 * kv_w + 2 * hg_w, attn_w + 2 * kv_w + 3 * hg_w]

    def proj(name, seg, n, epilogue, cols=(), outs=(), whole=False):
        tn = n if whole else _tile(n, 1024 if len(outs) == 1 else 512, LANES)
        while starts[seg] % tn:
            tn = _tile(n, tn - 1, LANES)
        assert tn == n or not whole
        tm_call = _tile(xn_p.shape[0], min(tm, KV_TM), 8) if whole else tm
        return _matmul(name, [(xn_p, w["w_in"], 0)], n=n, col_off=starts[seg], tm=tm_call, tn=tn, nk=1,
                       epilogue=epilogue, cols=cols, outs=outs, rider=([xn_s], []))

    res = []
    res += proj("proj_q", 0, attn_w, _epi_head_norm, cols=(w["q_gain"],), outs=(BF16,))
    res += proj("proj_k", 1, kv_w, _epi_head_norm, cols=(w["k_gain"],), outs=((F32, "heads"), BF16), whole=True)
    res += proj("proj_v", 2, kv_w, _epi_copy, outs=((F32, "heads"), BF16), whole=True)
    res += proj("proj_hq", 3, hg_w, _epi_silu, outs=(F32,))
    res += proj("proj_hf", 4, hg_w, functools.partial(_epi_forget, layer=0), cols=(w["lb_logits"],),
                outs=(F32, F32))
    res += proj("proj_hi", 5, hg_w, _epi_copy, outs=(BF16,))
    res += proj("proj_hg", 6, hg_w, _epi_sigmoid, outs=(F32,))
    counts = (1, 2, 2, 1, 2, 1, 1)
    prompt, sample, at = [], [], 0
    for c in counts:
        prompt += res[at:at + c]
        sample += res[at + c:at + 2 * c]
        at += 2 * c
    return dict(zip(_IN_NAMES, prompt)), dict(zip(_IN_NAMES, sample))


def _layer_outputs(xp, xs, attn_p, attn_s, hg_p, hg_s, w, *, tm):
    d = xp.shape[1]
    d_ff = w["w_up"].shape[1]
    aw = attn_p.shape[1]
    x1_p, x1_s = _matmul("proj_out", [(attn_p, w["w_out"], 0), (hg_p, w["w_out"], aw)], n=d, col_off=0, tm=tm,
                         tn=_tile(d, 512, LANES), nk=1, epilogue=_epi_residual, rows=(xp,), outs=(F32,),
                         rider=([attn_s, hg_s], [xs]))
    hn_p = _rmsnorm(x1_p, w["norm2_g"], _tile(xp.shape[0], 256, 8))
    hn_s = _rmsnorm(x1_s, w["norm2_g"], _tile(xs.shape[0], 256, 8))
    act_p, act_s = _matmul("mlp_up", [(hn_p, w["w_up"], 0)], n=d_ff, col_off=0, tm=tm, tn=_tile(d_ff, 512, LANES),
                           nk=1, epilogue=_epi_relu2, outs=(BF16,), rider=([hn_s], []))
    nk = d_ff // _tile(d_ff, MLP_DOWN_TK, LANES)
    (y_p,) = _matmul("mlp_down", [(act_p, w["w_down"], 0)], n=d, col_off=0, tm=tm, tn=_tile(d, 2048, LANES),
                     nk=nk, epilogue=_epi_residual, rows=(x1_p,), outs=(F32,), acc_in_out=True)
    (y_s,) = _matmul("mlp_down_s", [(act_s, w["w_down"], 0)], n=d, col_off=0, tm=_tile(xs.shape[0], 256, 8),
                     tn=_tile(d, 1024, LANES), nk=nk, epilogue=_epi_residual, rows=(x1_s,), outs=(F32,),
                     acc_in_out=True)
    return y_p, y_s


def kernel(x_prompt, x_sample, cache_k, cache_v, state_h, page_table, norm1_g, w_in, q_norm_g, k_norm_g,
           lb_logits, hg_norm_g, w_out, norm2_g, w_up, w_down):
    depth = w_in.shape[0]
    assert depth == 1, "one layer"
    bp, tp, d = x_prompt.shape
    bs, ts, _ = x_sample.shape
    attn_w = d // 2
    n_q = attn_w // HEAD_DIM
    n_kv = n_q // KV_GROUP
    kv_w = n_kv * HEAD_DIM
    hg_w = d - attn_w
    n_hg = hg_w // HEAD_DIM

    w = dict(
        norm1_g=norm1_g[0], norm2_g=norm2_g[0],
        w_in=w_in[0].astype(BF16), w_out=w_out[0], w_up=w_up[0],
        q_gain=jnp.tile(q_norm_g[0], n_q).reshape(1, attn_w),
        k_gain=jnp.tile(k_norm_g[0], n_kv).reshape(1, kv_w),
        lb_logits=lb_logits,
    )
    gnorm = hg_norm_g[0].reshape(1, hg_w)

    xp = x_prompt.reshape(bp * tp, d)
    xs = x_sample.reshape(bs * ts, d)
    tm = _tile(bp * tp, 1024, 8)
    gp, gs = _layer_inputs(xp, xs, w, tm=tm)

    attn_p, w_down_b = _moba_prompt(gp["q"], gp["kb"], gp["vb"], batch=bp, seq=tp, to_bf16=w_down[0])
    w["w_down"] = w_down[0].astype(BF16) if w_down_b is None else w_down_b
    h0_p = jnp.zeros((bp, n_hg, HEAD_DIM, HEAD_DIM), F32)
    hg_p, h_p = _hgrn(gp["hq"], gp["lf"], gp["hk"], gp["hv"], gp["gate"], gnorm, h0_p, batch=bp, seq=tp)
    attn_s = _moba_sample(gs["q"], gs["kb"], gs["vb"], cache_k[0].reshape(-1, HEAD_DIM),
                          cache_v[0].reshape(-1, HEAD_DIM), page_table, batch=bs, t_new=ts)
    hg_s, h_s = _hgrn(gs["hq"], gs["lf"], gs["hk"], gs["hv"], gs["gate"], gnorm, state_h[0], batch=bs, seq=ts)

    y_p, y_s = _layer_outputs(xp, xs, attn_p, attn_s, hg_p, hg_s, w, tm=tm)

    return (y_p.reshape(bp, tp, d), y_s.reshape(bs, ts, d),
            gp["k"].reshape(1, bp, tp, n_kv, HEAD_DIM), gp["v"].reshape(1, bp, tp, n_kv, HEAD_DIM),
            h_p.reshape(1, bp, n_hg, HEAD_DIM, HEAD_DIM),
            gs["k"].reshape(1, bs, ts, n_kv, HEAD_DIM), gs["v"].reshape(1, bs, ts, n_kv, HEAD_DIM),
            h_s.reshape(1, bs, n_hg, HEAD_DIM, HEAD_DIM))
```

```python
import functools

import jax
import jax.numpy as jnp
from jax import lax
from jax.experimental import pallas as pl
from jax.experimental.pallas import tpu as pltpu

F32 = jnp.float32
BF16 = jnp.bfloat16

HEAD_DIM = 128
KV_GROUP = 2
PAGE_SIZE = 128
MOBA_BLOCK = 256
MOBA_TOPK = 3
HG_CHUNK = 64
HG_SUB = 16
EPS = 1e-6
NEG = -1e30
LOG2E = 1.4426950408889634
LANES = 128
SUB_N = 512
MLP_DOWN_TK = 2048
KV_TM = 1024
VMEM_LIMIT = 60 * 1024 * 1024


def _tile(n, target, mult=1):
    for t in range(min(n, target), 0, -1):
        if n % t == 0 and t % mult == 0:
            return t
    return n


def _params(sem):
    return pltpu.CompilerParams(dimension_semantics=sem, vmem_limit_bytes=VMEM_LIMIT)


def _nt_dot(a, b):
    return lax.dot_general(a, b, (((1,), (1,)), ((), ())), preferred_element_type=F32)


def _tn_dot(a, b):
    return lax.dot_general(a, b, (((0,), (0,)), ((), ())), preferred_element_type=F32)


def _rmsnorm_body(x_ref, g_ref, o_ref):
    x = x_ref[...]
    ms = jnp.mean(x * x, axis=-1, keepdims=True)
    o_ref[...] = (x * lax.rsqrt(ms + EPS) * g_ref[...]).astype(o_ref.dtype)


def _rmsnorm(x, g, tm):
    m, d = x.shape
    return pl.pallas_call(
        _rmsnorm_body,
        grid=(m // tm,),
        in_specs=[pl.BlockSpec((tm, d), lambda i: (i, 0)), pl.BlockSpec((1, d), lambda i: (0, 0))],
        out_specs=pl.BlockSpec((tm, d), lambda i: (i, 0)),
        out_shape=jax.ShapeDtypeStruct((m, d), BF16),
        compiler_params=_params(("arbitrary",)),
        name="rmsnorm",
    )(x, g.reshape(1, d))


def _col_view(ref, c0, width):
    if len(ref.shape) == 3:
        return ref.at[:, c0 // HEAD_DIM:(c0 + width) // HEAD_DIM, :]
    return ref.at[:, c0:c0 + width]


def _matmul_tile(xs, ws, row_refs, col_refs, out_refs, acc_ref, *, nk, epilogue):
    tn = ws[0].shape[1]
    if nk == 1:
        sub = SUB_N if tn % SUB_N == 0 else tn
        for c0 in range(0, tn, sub):
            part = None
            for x_ref, w_ref in zip(xs, ws):
                d = jnp.dot(x_ref[...], w_ref[:, c0:c0 + sub].astype(BF16), preferred_element_type=F32)
                part = d if part is None else part + d
            epilogue(part, [_col_view(r, c0, sub) for r in row_refs], [_col_view(r, c0, sub) for r in col_refs],
                     [_col_view(r, c0, sub) for r in out_refs])
        return
    k = pl.program_id(2)

    @pl.when(k == 0)
    def _():
        acc_ref[...] = jnp.zeros(acc_ref.shape, F32)

    part = acc_ref[...]
    for x_ref, w_ref in zip(xs, ws):
        part = part + jnp.dot(x_ref[...], w_ref[...].astype(BF16), preferred_element_type=F32)
    acc_ref[...] = part

    @pl.when(k == nk - 1)
    def _():
        epilogue(acc_ref[...], row_refs, col_refs, out_refs)


def _matmul_body(*refs, n_pairs, n_rows, n_cols, n_outs, nk, epilogue, rider, acc_in_out):
    it = iter(refs)
    take = lambda cnt: [next(it) for _ in range(cnt)]
    xw = take(2 * n_pairs)
    xs, ws = xw[0::2], xw[1::2]
    row_refs, col_refs = take(n_rows), take(n_cols)
    xs2, row_refs2 = (take(n_pairs), take(n_rows)) if rider else ([], [])
    out_refs = take(n_outs)
    out_refs2 = take(n_outs) if rider else []
    acc_ref = out_refs[0] if acc_in_out else (next(it) if nk > 1 else None)
    acc_ref2 = next(it) if rider and nk > 1 else None
    _matmul_tile(xs, ws, row_refs, col_refs, out_refs, acc_ref, nk=nk, epilogue=epilogue)
    if rider:
        @pl.when(pl.program_id(0) == 0)
        def _():
            _matmul_tile(xs2, ws, row_refs2, col_refs, out_refs2, acc_ref2, nk=nk, epilogue=epilogue)


def _matmul(name, pairs, *, n, col_off, tm, tn, nk, epilogue, rows=(), cols=(), outs=(), rider=None,
            acc_in_out=False):
    m = pairs[0][0].shape[0]
    assert m % tm == 0 and n % tn == 0 and col_off % tn == 0
    joff = col_off // tn
    nj = n // tn
    first_j = lambda i, j: jnp.where(i == 0, j, nj - 1)
    first_k = lambda i, k: jnp.where(i == 0, k, nk - 1)
    in_specs, args = [], []
    for x, w, row_off in pairs:
        kdim = x.shape[1]
        assert kdim % nk == 0
        tk = kdim // nk
        assert row_off % tk == 0
        in_specs += [pl.BlockSpec((tm, tk), lambda i, j, k: (i, k)),
                     pl.BlockSpec((tk, tn), lambda i, j, k, koff=row_off // tk: (k + koff, j + joff))]
        args += [x, w]
    for r in rows:
        in_specs.append(pl.BlockSpec((tm, tn), lambda i, j, k: (i, j)))
        args.append(r)
    for c in cols:
        in_specs.append(pl.BlockSpec((c.shape[0], tn), lambda i, j, k: (0, j)))
        args.append(c)
    m2 = 0
    if rider:
        xs2, rows2 = rider
        m2 = xs2[0].shape[0]
        for x2, (x, _, _) in zip(xs2, pairs):
            assert x2.shape[1] == x.shape[1]
            in_specs.append(pl.BlockSpec((m2, x.shape[1] // nk), lambda i, j, k: (0, first_k(i, k))))
            args.append(x2)
        for r in rows2:
            in_specs.append(pl.BlockSpec((m2, tn), lambda i, j, k: (0, first_j(i, j))))
            args.append(r)
    out_specs, out_shape = [], []

    def add_outs(mrows, trows, imap, jmap):
        for o in outs:
            if isinstance(o, tuple):
                out_specs.append(pl.BlockSpec((trows, tn // HEAD_DIM, HEAD_DIM),
                                              lambda i, j, k: (imap(i), jmap(i, j), 0)))
                out_shape.append(jax.ShapeDtypeStruct((mrows, n // HEAD_DIM, HEAD_DIM), o[0]))
            else:
                out_specs.append(pl.BlockSpec((trows, tn), lambda i, j, k: (imap(i), jmap(i, j))))
                out_shape.append(jax.ShapeDtypeStruct((mrows, n), o))

    add_outs(m, tm, lambda i: i, lambda i, j: j)
    if rider:
        add_outs(m2, m2, lambda i: 0, first_j)
    scratch = []
    if nk > 1:
        assert not acc_in_out or (len(outs) == 1 and outs[0] == F32)
        scratch = ([] if acc_in_out else [pltpu.VMEM((tm, tn), F32)]) + ([pltpu.VMEM((m2, tn), F32)] if rider else [])
    body = functools.partial(_matmul_body, n_pairs=len(pairs), n_rows=len(rows), n_cols=len(cols),
                             n_outs=len(outs), nk=nk, epilogue=epilogue, rider=bool(rider),
                             acc_in_out=acc_in_out and nk > 1)
    return pl.pallas_call(
        body,
        grid=(m // tm, nj, nk),
        in_specs=in_specs,
        out_specs=out_specs,
        out_shape=out_shape,
        scratch_shapes=scratch,
        compiler_params=_params(("arbitrary", "arbitrary", "arbitrary")),
        name=name,
    )(*args)


def _store_head(o, h, y):
    if len(o.shape) == 3:
        o[:, h, :] = y.astype(o.dtype)
    else:
        o[:, h * HEAD_DIM:(h + 1) * HEAD_DIM] = y.astype(o.dtype)


def _epi_head_norm(acc, rows, cols, outs):
    g = cols[0][...]
    for h in range(acc.shape[1] // HEAD_DIM):
        sl = slice(h * HEAD_DIM, (h + 1) * HEAD_DIM)
        blk = acc[:, sl]
        y = blk * lax.rsqrt(jnp.mean(blk * blk, axis=-1, keepdims=True) + EPS) * g[:, sl]
        for o in outs:
            _store_head(o, h, y)


def _epi_copy(acc, rows, cols, outs):
    for h in range(acc.shape[1] // HEAD_DIM):
        for o in outs:
            _store_head(o, h, acc[:, h * HEAD_DIM:(h + 1) * HEAD_DIM])


def _epi_silu(acc, rows, cols, outs):
    outs[0][...] = (acc * jax.nn.sigmoid(acc)).astype(outs[0].dtype)


def _epi_sigmoid(acc, rows, cols, outs):
    outs[0][...] = jax.nn.sigmoid(acc).astype(outs[0].dtype)


def _epi_forget(acc, rows, cols, outs, *, layer):
    logits = cols[0][...]
    e = jnp.exp(logits - jnp.max(logits, axis=0, keepdims=True))
    lb = jnp.sum(e[:layer + 1], axis=0, keepdims=True) / jnp.sum(e, axis=0, keepdims=True)
    forget = lb + (1.0 - lb) * jax.nn.sigmoid(acc)
    outs[0][...] = jnp.log(forget)
    outs[1][...] = 1.0 - forget


def _epi_residual(acc, rows, cols, outs):
    outs[0][...] = rows[0][...] + acc


def _epi_relu2(acc, rows, cols, outs):
    r = jnp.maximum(acc, 0.0)
    outs[0][...] = (r * r).astype(outs[0].dtype)


def _select_blocks_t(sc, n_valid, topk):
    nb = sc.shape[0]
    row = lax.broadcasted_iota(jnp.int32, sc.shape, 0)
    valid = row < n_valid
    sel = jnp.zeros(sc.shape, F32)
    for j in range(nb):
        sj = sc[j:j + 1, :]
        beats = ((sc > sj) | ((sc == sj) & (row < j))) & valid
        rank = jnp.sum(jnp.where(beats, 1.0, 0.0), axis=0, keepdims=True)
        chosen = jnp.where((rank < float(topk)) & (j < n_valid), 1.0, 0.0)
        sel = jnp.where(row == j, chosen, sel)
    return sel


def _moba_prompt_tile(i, q_ref, k_ref, vt_sc, o_ref, kmean, *, scale):
    blk = MOBA_BLOCK
    rows = KV_GROUP * blk
    qb = q_ref[i * blk:(i + 1) * blk, :]
    q2 = jnp.concatenate([qb[:, g * HEAD_DIM:(g + 1) * HEAD_DIM] for g in range(KV_GROUP)], axis=0)
    sel_t = _select_blocks_t(_nt_dot(kmean, q2), i, MOBA_TOPK)
    bias_t = jnp.where(sel_t > 0.5, 0.0, NEG)

    def scores(j):
        s = _nt_dot(k_ref[j * blk:(j + 1) * blk, :], q2)
        if j == i:
            kpos = lax.broadcasted_iota(jnp.int32, s.shape, 0)
            qpos = lax.broadcasted_iota(jnp.int32, s.shape, 1) % blk
            return jnp.where(kpos <= qpos, s, NEG)
        return s + bias_t[j:j + 1, :]

    blocks = list(range(i + 1))
    mrun = scores(i)
    for j in blocks[:-1]:
        mrun = jnp.maximum(mrun, scores(j))
    m = jnp.max(mrun, axis=0, keepdims=True)
    lsum = None
    acc = None
    for j in blocks:
        pe = jnp.exp2((scores(j) - m) * (scale * LOG2E))
        lsum = pe if lsum is None else lsum + pe
        d = jnp.dot(vt_sc[:, j * blk:(j + 1) * blk], pe.astype(BF16), preferred_element_type=F32)
        acc = d if acc is None else acc + d
    o = (acc / jnp.sum(lsum, axis=0, keepdims=True)).T
    for g in range(KV_GROUP):
        o_ref[i * blk:(i + 1) * blk, g * HEAD_DIM:(g + 1) * HEAD_DIM] = o[g * blk:(g + 1) * blk].astype(o_ref.dtype)


def _moba_prompt_body(q_ref, k_ref, v_ref, *rest, nb, scale, side_cast):
    if side_cast:
        w_ref, o_ref, wb_ref, kmean_sc, vt_sc = rest
        wb_ref[...] = w_ref[...].astype(BF16)
    else:
        o_ref, kmean_sc, vt_sc = rest
    p = pl.program_id(2)

    @pl.when(p == 0)
    def _():
        kall = k_ref[...].astype(F32)
        kmean_sc[...] = jnp.sum(kall.reshape(nb, MOBA_BLOCK, HEAD_DIM), axis=1) * (1.0 / MOBA_BLOCK)
        vt_sc[...] = v_ref[...].astype(F32).T.astype(BF16)

    for c in range(nb // 2):
        @pl.when(p == c)
        def _(c=c):
            kmean = kmean_sc[...].astype(BF16)
            for i in (c, nb - 1 - c):
                _moba_prompt_tile(i, q_ref, k_ref, vt_sc, o_ref, kmean, scale=scale)


def _moba_prompt(q, kb, vb, *, batch, seq, to_bf16=None):
    assert seq % (2 * MOBA_BLOCK) == 0, "query tiles are processed in (p, nb-1-p) pairs"
    nb = seq // MOBA_BLOCK
    assert nb <= LANES
    n_kv = kb.shape[1] // HEAD_DIM
    gw = KV_GROUP * HEAD_DIM
    npair = nb // 2
    steps = batch * n_kv * npair
    side_cast = to_bf16 is not None and to_bf16.shape[0] % (steps * 16) == 0
    in_specs = [pl.BlockSpec((seq, gw), lambda b, n, p: (b, n)),
                pl.BlockSpec((seq, HEAD_DIM), lambda b, n, p: (b, n)),
                pl.BlockSpec((seq, HEAD_DIM), lambda b, n, p: (b, n))]
    out_specs = [pl.BlockSpec((seq, gw), lambda b, n, p: (b, n))]
    out_shape = [jax.ShapeDtypeStruct(q.shape, BF16)]
    args = [q, kb, vb]
    if side_cast:
        slab = pl.BlockSpec((to_bf16.shape[0] // steps, to_bf16.shape[1]),
                            lambda b, n, p: ((b * n_kv + n) * npair + p, 0))
        in_specs.append(slab)
        out_specs.append(slab)
        out_shape.append(jax.ShapeDtypeStruct(to_bf16.shape, BF16))
        args.append(to_bf16)
    body = functools.partial(_moba_prompt_body, nb=nb, scale=HEAD_DIM ** -0.5, side_cast=side_cast)
    res = pl.pallas_call(
        body,
        grid=(batch, n_kv, npair),
        in_specs=in_specs,
        out_specs=out_specs,
        out_shape=out_shape,
        scratch_shapes=[pltpu.VMEM((nb, HEAD_DIM), F32),
                        pltpu.VMEM((HEAD_DIM, seq), BF16)],
        compiler_params=_params(("arbitrary", "arbitrary", "arbitrary")),
        name="moba_prompt",
    )(*args)
    return res[0], (res[1] if side_cast else None)


def _moba_sample_body(pt_ref, q_ref, kn_ref, vn_ref, *rest, pps, nch, n_kv, t_new, n_pages, scale):
    k_refs = rest[:pps]
    v_refs = rest[pps:2 * pps]
    o_ref = rest[2 * pps]
    qbd_sc, s_sc, p_sc, km_sc, oacc_sc, l_sc = rest[2 * pps + 1:]
    c = pl.program_id(1)
    ppb = MOBA_BLOCK // PAGE_SIZE
    nb = n_pages // ppb
    gt = KV_GROUP * t_new
    rows = n_kv * gt
    kvw = n_kv * HEAD_DIM

    @pl.when(c == 0)
    def _():
        qv = q_ref[...].astype(F32)
        per_g = [jnp.concatenate([qv[:, (KV_GROUP * n + g) * HEAD_DIM:(KV_GROUP * n + g + 1) * HEAD_DIM]
                                  for n in range(n_kv)], axis=1) for g in range(KV_GROUP)]
        tiled = jnp.concatenate([per_g[g] for n in range(n_kv) for g in range(KV_GROUP)], axis=0)
        rown = lax.broadcasted_iota(jnp.int32, (rows, kvw), 0) // gt
        coln = lax.broadcasted_iota(jnp.int32, (rows, kvw), 1) // HEAD_DIM
        qbd_sc[...] = jnp.where(rown == coln, tiled, 0.0).astype(BF16)

    def page_rows(ref):
        n_tok = ref.shape[0] // n_kv
        return jnp.concatenate([ref[pl.ds(n, n_tok, stride=n_kv), :] for n in range(n_kv)], axis=1)

    @pl.when(c < nch)
    def _():
        qbd = qbd_sc[...]
        colsum = None
        for i in range(pps):
            cs = jnp.sum(k_refs[i][...].reshape(PAGE_SIZE, n_kv, HEAD_DIM), axis=0)
            colsum = cs if i % ppb == 0 else colsum + cs
            if i % ppb == ppb - 1:
                km_sc[pl.ds(c * (pps // ppb) + i // ppb, 1)] = colsum[None]
            col0 = pl.multiple_of((c * pps + i) * PAGE_SIZE, PAGE_SIZE)
            s_sc[:, pl.ds(col0, PAGE_SIZE)] = _nt_dot(qbd, page_rows(k_refs[i]).astype(BF16))

    @pl.when(c == nch - 1)
    def _():
        qbd = qbd_sc[...]
        kmean = (jnp.concatenate([km_sc[:, n, :] for n in range(n_kv)], axis=1)
                 * (1.0 / MOBA_BLOCK)).astype(BF16)
        sc_t = _nt_dot(kmean, qbd)
        sel_t = _select_blocks_t(sc_t, nb, MOBA_TOPK)
        bias = jnp.where(sel_t > 0.5, 0.0, NEG).T
        zpad = jnp.zeros((LANES - t_new, kvw), F32)
        knp = jnp.concatenate([kn_ref[...].astype(F32), zpad], axis=0).astype(BF16)
        vnp = jnp.concatenate([vn_ref[...].astype(F32), zpad], axis=0).astype(BF16)
        s_own = _nt_dot(qbd, knp)
        tok = lax.broadcasted_iota(jnp.int32, s_own.shape, 0) % t_new
        key = lax.broadcasted_iota(jnp.int32, s_own.shape, 1)
        s_own = jnp.where(key <= tok, s_own, NEG)
        m = jnp.max(s_own, axis=-1, keepdims=True)
        for j in range(nb):
            sl = slice(j * MOBA_BLOCK, (j + 1) * MOBA_BLOCK)
            sb = s_sc[:, sl] + bias[:, j:j + 1]
            s_sc[:, sl] = sb
            m = jnp.maximum(m, jnp.max(sb, axis=-1, keepdims=True))
        p_own = jnp.exp((s_own - m) * scale)
        l = jnp.sum(p_own, axis=-1, keepdims=True)
        for j in range(nb):
            sl = slice(j * MOBA_BLOCK, (j + 1) * MOBA_BLOCK)
            p = jnp.exp((s_sc[:, sl] - m) * scale)
            l = l + jnp.sum(p, axis=-1, keepdims=True)
            p_sc[:, sl] = p.astype(BF16)
        l_sc[...] = jnp.broadcast_to(l, l_sc.shape)
        oacc_sc[...] = jnp.dot(p_own.astype(BF16), vnp, preferred_element_type=F32)

    @pl.when(c >= nch)
    def _():
        acc = oacc_sc[...]
        for i in range(pps):
            col0 = pl.multiple_of(((c - nch) * pps + i) * PAGE_SIZE, PAGE_SIZE)
            acc = acc + jnp.dot(p_sc[:, pl.ds(col0, PAGE_SIZE)], page_rows(v_refs[i]).astype(BF16),
                                preferred_element_type=F32)
        oacc_sc[...] = acc

    @pl.when(c == 2 * nch - 1)
    def _():
        o = oacc_sc[...] / l_sc[:, :1]
        for n in range(n_kv):
            for g in range(KV_GROUP):
                r0 = n * gt + g * t_new
                h = KV_GROUP * n + g
                o_ref[:, h * HEAD_DIM:(h + 1) * HEAD_DIM] = (
                    o[r0:r0 + t_new, n * HEAD_DIM:(n + 1) * HEAD_DIM].astype(o_ref.dtype))


def _moba_sample(q, kn, vn, cache_k, cache_v, page_table, *, batch, t_new):
    n_pages = page_table.shape[1]
    ppb = MOBA_BLOCK // PAGE_SIZE
    assert n_pages % ppb == 0 and n_pages // ppb >= MOBA_TOPK
    kvw = kn.shape[1]
    n_kv = kvw // HEAD_DIM
    pps = _tile(n_pages, 16, ppb)
    nch = n_pages // pps
    rows = n_kv * KV_GROUP * t_new
    past = n_pages * PAGE_SIZE

    def k_map(i):
        return lambda b, c, pt: (pt[b, jnp.minimum(c, nch - 1) * pps + i], 0)

    def v_map(i):
        return lambda b, c, pt: (pt[b, jnp.maximum(c - nch, 0) * pps + i], 0)

    in_specs = [pl.BlockSpec((t_new, q.shape[1]), lambda b, c, pt: (b, 0)),
                pl.BlockSpec((t_new, kvw), lambda b, c, pt: (b, 0)),
                pl.BlockSpec((t_new, kvw), lambda b, c, pt: (b, 0))]
    in_specs += [pl.BlockSpec((PAGE_SIZE * n_kv, HEAD_DIM), k_map(i)) for i in range(pps)]
    in_specs += [pl.BlockSpec((PAGE_SIZE * n_kv, HEAD_DIM), v_map(i)) for i in range(pps)]
    body = functools.partial(_moba_sample_body, pps=pps, nch=nch, n_kv=n_kv, t_new=t_new,
                             n_pages=n_pages, scale=HEAD_DIM ** -0.5)
    grid_spec = pltpu.PrefetchScalarGridSpec(
        num_scalar_prefetch=1,
        grid=(batch, 2 * nch),
        in_specs=in_specs,
        out_specs=pl.BlockSpec((t_new, q.shape[1]), lambda b, c, pt: (b, 0)),
        scratch_shapes=[pltpu.VMEM((rows, kvw), BF16),
                        pltpu.VMEM((rows, past), F32),
                        pltpu.VMEM((rows, past), BF16),
                        pltpu.VMEM((n_pages // ppb, n_kv, HEAD_DIM), F32),
                        pltpu.VMEM((rows, kvw), F32),
                        pltpu.VMEM((rows, LANES), F32)],
    )
    return pl.pallas_call(
        body,
        grid_spec=grid_spec,
        out_shape=jax.ShapeDtypeStruct(q.shape, BF16),
        compiler_params=_params(("arbitrary", "arbitrary")),
        name="moba_sample",
    )(page_table, q, kn, vn, *([cache_k] * pps), *([cache_v] * pps))


def _hgrn_body(q_ref, lf_ref, k_ref, v_ref, gate_ref, gn_ref, h0_ref, o_ref, hout_ref, ht_sc,
               *, hb, chunk, sub, n_chunks, n_tsteps):
    t = pl.program_id(2)
    ns = chunk // sub

    @pl.when(t == 0)
    def _():
        for h in range(hb):
            ht_sc[h] = h0_ref[h].T

    r_i = lax.broadcasted_iota(jnp.int32, (chunk, chunk), 0)
    c_i = lax.broadcasted_iota(jnp.int32, (chunk, chunk), 1)
    causal = c_i <= r_i
    tri = jnp.where(causal, 1.0, 0.0).astype(BF16)
    width = hb * HEAD_DIM
    rowc = lax.broadcasted_iota(jnp.int32, (chunk, width), 0)
    heads = [slice(h * HEAD_DIM, (h + 1) * HEAD_DIM) for h in range(hb)]

    def one_chunk(ci, carry):
        rs = pl.ds(pl.multiple_of(ci * chunk, chunk), chunk)
        lf = lf_ref[rs, :]
        q = q_ref[rs, :]
        k = k_ref[rs, :]
        v = v_ref[rs, :]
        hi = lf.astype(BF16)
        lo = (lf - hi.astype(F32)).astype(BF16)
        bb = jnp.dot(tri, jnp.concatenate([hi, lo], axis=1), preferred_element_type=F32)
        b = bb[:, :width] + bb[:, width:]
        b_last = b[chunk - 1:chunk, :]
        qe = (q * jnp.exp(b)).astype(BF16)
        kd = (k * jnp.exp(b_last - b)).astype(BF16)
        decay = jnp.exp(b_last)
        refs = [jnp.zeros((1, width), F32)] + [b[sub * s - 1:sub * s, :] for s in range(1, ns)]
        gfull = jnp.concatenate([jnp.broadcast_to(r, (sub, width)) for r in refs], axis=0)
        qh = q * jnp.exp(b - gfull)
        qparts = [jnp.where((rowc >= sub * s) & (rowc < sub * (s + 1)), qh, 0.0).astype(BF16) for s in range(ns)]
        kparts = [jnp.where(rowc < sub * (s + 1), k * jnp.exp(refs[s] - b), 0.0).astype(BF16) for s in range(ns)]
        hts = [ht_sc[h] for h in range(hb)]
        o_state = [_nt_dot(qe[:, cs], hts[h].astype(BF16)) for h, cs in enumerate(heads)]
        att = [_nt_dot(jnp.concatenate([p[:, cs] for p in qparts], axis=1),
                       jnp.concatenate([p[:, cs] for p in kparts], axis=1)) for cs in heads]
        upd = [_tn_dot(v[:, cs], kd[:, cs]) for cs in heads]
        o_att = [jnp.dot(jnp.where(causal, att[h], 0.0).astype(BF16), v[:, cs], preferred_element_type=F32)
                 for h, cs in enumerate(heads)]
        for h, cs in enumerate(heads):
            ht_sc[h] = hts[h] * decay[:, cs] + upd[h]
            o = o_state[h] + o_att[h]
            y = o * lax.rsqrt(jnp.mean(o * o, axis=-1, keepdims=True) + EPS) * gn_ref[:, cs]
            o_ref[rs, cs] = (y * gate_ref[rs, cs]).astype(o_ref.dtype)
        return carry

    lax.fori_loop(0, n_chunks, one_chunk, 0)

    @pl.when(t == n_tsteps - 1)
    def _():
        for h in range(hb):
            hout_ref[h] = ht_sc[h].T


def _hgrn(hq, lf, hk, hv, gate, gnorm, h0, *, batch, seq):
    width = hq.shape[1]
    n_heads = width // HEAD_DIM
    hb = _tile(n_heads, 16)
    chunk = HG_CHUNK if seq % HG_CHUNK == 0 else seq
    sub = HG_SUB if chunk % HG_SUB == 0 else chunk
    tc = _tile(seq, 256, chunk)
    nt = seq // tc
    bw = hb * HEAD_DIM
    row_spec = pl.BlockSpec((tc, bw), lambda b, g, t: (b * nt + t, g))
    st_spec = pl.BlockSpec((None, hb, HEAD_DIM, HEAD_DIM), lambda b, g, t: (b, g, 0, 0))
    body = functools.partial(_hgrn_body, hb=hb, chunk=chunk, sub=sub, n_chunks=tc // chunk, n_tsteps=nt)
    return pl.pallas_call(
        body,
        grid=(batch, n_heads // hb, nt),
        in_specs=[row_spec, row_spec, row_spec, row_spec, row_spec,
                  pl.BlockSpec((1, bw), lambda b, g, t: (0, g)), st_spec],
        out_specs=[row_spec, st_spec],
        out_shape=[jax.ShapeDtypeStruct(hq.shape, BF16), jax.ShapeDtypeStruct(h0.shape, F32)],
        scratch_shapes=[pltpu.VMEM((hb, HEAD_DIM, HEAD_DIM), F32)],
        compiler_params=_params(("arbitrary", "arbitrary", "arbitrary")),
        name="hgrn",
    )(hq, lf, hk, hv, gate, gnorm, h0)


_IN_NAMES = ("q", "k", "kb", "v", "vb", "hq", "lf", "hk", "hv", "gate")


def _layer_inputs(xp, xs, w, *, tm):
    d = xp.shape[1]
    attn_w = d // 2
    kv_w = attn_w // KV_GROUP
    hg_w = d - attn_w
    xn_p = _rmsnorm(xp, w["norm1_g"], _tile(xp.shape[0], 256, 8))
    xn_s = _rmsnorm(xs, w["norm1_g"], _tile(xs.shape[0], 256, 8))
    starts = [0, attn_w, attn_w + kv_w, attn_w + 2 * kv_w, attn_w + 2 * kv_w + hg_w,
              attn_w + 2 * kv_w + 2 * hg_w, attn_w + 2 * kv_w + 3 * hg_w]

    def proj(name, seg, n, epilogue, cols=(), outs=(), whole=False):
        tn = n if whole else _tile(n, 1024 if len(outs) == 1 else 512, LANES)
        while starts[seg] % tn:
            tn = _tile(n, tn - 1, LANES)
        assert tn == n or not whole
        tm_call = _tile(xn_p.shape[0], min(tm, KV_TM), 8) if whole else tm
        return _matmul(name, [(xn_p, w["w_in"], 0)], n=n, col_off=starts[seg], tm=tm_call, tn=tn, nk=1,
                       epilogue=epilogue, cols=cols, outs=outs, rider=([xn_s], []))

    res = []
    res += proj("proj_q", 0, attn_w, _epi_head_norm, cols=(w["q_gain"],), outs=(BF16,))
    res += proj("proj_k", 1, kv_w, _epi_head_norm, cols=(w["k_gain"],), outs=((F32, "heads"), BF16), whole=True)
    res += proj("proj_v", 2, kv_w, _epi_copy, outs=((F32, "heads"), BF16), whole=True)
    res += proj("proj_hq", 3, hg_w, _epi_silu, outs=(F32,))
    res += proj("proj_hf", 4, hg_w, functools.partial(_epi_forget, layer=0), cols=(w["lb_logits"],),
                outs=(F32, F32))
    res += proj("proj_hi", 5, hg_w, _epi_copy, outs=(BF16,))
    res += proj("proj_hg", 6, hg_w, _epi_sigmoid, outs=(F32,))
    counts = (1, 2, 2, 1, 2, 1, 1)
    prompt, sample, at = [], [], 0
    for c in counts:
        prompt += res[at:at + c]
        sample += res[at + c:at + 2 * c]
        at += 2 * c
    return dict(zip(_IN_NAMES, prompt)), dict(zip(_IN_NAMES, sample))


def _layer_outputs(xp, xs, attn_p, attn_s, hg_p, hg_s, w, *, tm):
    d = xp.shape[1]
    d_ff = w["w_up"].shape[1]
    aw = attn_p.shape[1]
    x1_p, x1_s = _matmul("proj_out", [(attn_p, w["w_out"], 0), (hg_p, w["w_out"], aw)], n=d, col_off=0, tm=tm,
                         tn=_tile(d, 512, LANES), nk=1, epilogue=_epi_residual, rows=(xp,), outs=(F32,),
                         rider=([attn_s, hg_s], [xs]))
    hn_p = _rmsnorm(x1_p, w["norm2_g"], _tile(xp.shape[0], 256, 8))
    hn_s = _rmsnorm(x1_s, w["norm2_g"], _tile(xs.shape[0], 256, 8))
    act_p, act_s = _matmul("mlp_up", [(hn_p, w["w_up"], 0)], n=d_ff, col_off=0, tm=tm, tn=_tile(d_ff, 512, LANES),
                           nk=1, epilogue=_epi_relu2, outs=(BF16,), rider=([hn_s], []))
    nk = d_ff // _tile(d_ff, MLP_DOWN_TK, LANES)
    y_p, y_s = _matmul("mlp_down", [(act_p, w["w_down"], 0)], n=d, col_off=0, tm=tm, tn=_tile(d, 1024, LANES),
                       nk=nk, epilogue=_epi_residual, rows=(x1_p,), outs=(F32,), rider=([act_s], [x1_s]),
                       acc_in_out=True)
    return y_p, y_s


def kernel(x_prompt, x_sample, cache_k, cache_v, state_h, page_table, norm1_g, w_in, q_norm_g, k_norm_g,
           lb_logits, hg_norm_g, w_out, norm2_g, w_up, w_down):
    depth = w_in.shape[0]
    assert depth == 1, "one layer"
    bp, tp, d = x_prompt.shape
    bs, ts, _ = x_sample.shape
    attn_w = d // 2
    n_q = attn_w // HEAD_DIM
    n_kv = n_q // KV_GROUP
    kv_w = n_kv * HEAD_DIM
    hg_w = d - attn_w
    n_hg = hg_w // HEAD_DIM

    w = dict(
        norm1_g=norm1_g[0], norm2_g=norm2_g[0],
        w_in=w_in[0].astype(BF16), w_out=w_out[0], w_up=w_up[0],
        q_gain=jnp.tile(q_norm_g[0], n_q).reshape(1, attn_w),
        k_gain=jnp.tile(k_norm_g[0], n_kv).reshape(1, kv_w),
        lb_logits=lb_logits,
    )
    gnorm = hg_norm_g[0].reshape(1, hg_w)

    xp = x_prompt.reshape(bp * tp, d)
    xs = x_sample.reshape(bs * ts, d)
    tm = _tile(bp * tp, 1024, 8)
    gp, gs = _layer_inputs(xp, xs, w, tm=tm)

    attn_p, w_down_b = _moba_prompt(gp["q"], gp["kb"], gp["vb"], batch=bp, seq=tp, to_bf16=w_down[0])
    w["w_down"] = w_down[0].astype(BF16) if w_down_b is None else w_down_b
    h0_p = jnp.zeros((bp, n_hg, HEAD_DIM, HEAD_DIM), F32)
    hg_p, h_p = _hgrn(gp["hq"], gp["lf"], gp["hk"], gp["hv"], gp["gate"], gnorm, h0_p, batch=bp, seq=tp)
    attn_s = _moba_sample(gs["q"], gs["kb"], gs["vb"], cache_k[0].reshape(-1, HEAD_DIM),
                          cache_v[0].reshape(-1, HEAD_DIM), page_table, batch=bs, t_new=ts)
    hg_s, h_s = _hgrn(gs["hq"], gs["lf"], gs["hk"], gs["hv"], gs["gate"], gnorm, state_h[0], batch=bs, seq=ts)

    y_p, y_s = _layer_outputs(xp, xs, attn_p, attn_s, hg_p, hg_s, w, tm=tm)

    return (y_p.reshape(bp, tp, d), y_s.reshape(bs, ts, d),
            gp["k"].reshape(1, bp, tp, n_kv, HEAD_DIM), gp["v"].reshape(1, bp, tp, n_kv, HEAD_DIM),
            h_p.reshape(1, bp, n_hg, HEAD_DIM, HEAD_DIM),
            gs["k"].reshape(1, bs, ts, n_kv, HEAD_DIM), gs["v"].reshape(1, bs, ts, n_kv, HEAD_DIM),
            h_s.reshape(1, bs, n_hg, HEAD_DIM, HEAD_DIM))
```

```python
import functools

import jax
import jax.numpy as jnp
from jax import lax
from jax.experimental import pallas as pl
from jax.experimental.pallas import tpu as pltpu

F32 = jnp.float32
BF16 = jnp.bfloat16

HEAD_DIM = 128
KV_GROUP = 2
PAGE_SIZE = 128
MOBA_BLOCK = 256
MOBA_TOPK = 3
HG_CHUNK = 64
HG_SUB = 16
EPS = 1e-6
NEG = -1e30
LOG2E = 1.4426950408889634
LANES = 128
SUB_N = 512
MLP_DOWN_TK = 2048
KV_TM = 1024
VMEM_LIMIT = 60 * 1024 * 1024


def _tile(n, target, mult=1):
    for t in range(min(n, target), 0, -1):
        if n % t == 0 and t % mult == 0:
            return t
    return n


def _params(sem):
    return pltpu.CompilerParams(dimension_semantics=sem, vmem_limit_bytes=VMEM_LIMIT)


def _nt_dot(a, b):
    return lax.dot_general(a, b, (((1,), (1,)), ((), ())), preferred_element_type=F32)


def _tn_dot(a, b):
    return lax.dot_general(a, b, (((0,), (0,)), ((), ())), preferred_element_type=F32)


def _rmsnorm_body(x_ref, g_ref, o_ref):
    x = x_ref[...]
    ms = jnp.mean(x * x, axis=-1, keepdims=True)
    o_ref[...] = (x * lax.rsqrt(ms + EPS) * g_ref[...]).astype(o_ref.dtype)


def _rmsnorm(x, g, tm):
    m, d = x.shape
    return pl.pallas_call(
        _rmsnorm_body,
        grid=(m // tm,),
        in_specs=[pl.BlockSpec((tm, d), lambda i: (i, 0)), pl.BlockSpec((1, d), lambda i: (0, 0))],
        out_specs=pl.BlockSpec((tm, d), lambda i: (i, 0)),
        out_shape=jax.ShapeDtypeStruct((m, d), BF16),
        compiler_params=_params(("arbitrary",)),
        name="rmsnorm",
    )(x, g.reshape(1, d))


def _col_view(ref, c0, width):
    if len(ref.shape) == 3:
        return ref.at[:, c0 // HEAD_DIM:(c0 + width) // HEAD_DIM, :]
    return ref.at[:, c0:c0 + width]


def _matmul_tile(xs, ws, row_refs, col_refs, out_refs, acc_ref, *, nk, epilogue):
    tn = ws[0].shape[1]
    if nk == 1:
        sub = SUB_N if tn % SUB_N == 0 else tn
        for c0 in range(0, tn, sub):
            part = None
            for x_ref, w_ref in zip(xs, ws):
                d = jnp.dot(x_ref[...], w_ref[:, c0:c0 + sub].astype(BF16), preferred_element_type=F32)
                part = d if part is None else part + d
            epilogue(part, [_col_view(r, c0, sub) for r in row_refs], [_col_view(r, c0, sub) for r in col_refs],
                     [_col_view(r, c0, sub) for r in out_refs])
        return
    k = pl.program_id(2)

    @pl.when(k == 0)
    def _():
        acc_ref[...] = jnp.zeros(acc_ref.shape, F32)

    part = acc_ref[...]
    for x_ref, w_ref in zip(xs, ws):
        part = part + jnp.dot(x_ref[...], w_ref[...].astype(BF16), preferred_element_type=F32)
    acc_ref[...] = part

    @pl.when(k == nk - 1)
    def _():
        epilogue(acc_ref[...], row_refs, col_refs, out_refs)


def _matmul_body(*refs, n_pairs, n_rows, n_cols, n_outs, nk, epilogue, rider, acc_in_out):
    it = iter(refs)
    take = lambda cnt: [next(it) for _ in range(cnt)]
    xw = take(2 * n_pairs)
    xs, ws = xw[0::2], xw[1::2]
    row_refs, col_refs = take(n_rows), take(n_cols)
    xs2, row_refs2 = (take(n_pairs), take(n_rows)) if rider else ([], [])
    out_refs = take(n_outs)
    out_refs2 = take(n_outs) if rider else []
    acc_ref = out_refs[0] if acc_in_out else (next(it) if nk > 1 else None)
    acc_ref2 = next(it) if rider and nk > 1 else None
    _matmul_tile(xs, ws, row_refs, col_refs, out_refs, acc_ref, nk=nk, epilogue=epilogue)
    if rider:
        @pl.when(pl.program_id(0) == 0)
        def _():
            _matmul_tile(xs2, ws, row_refs2, col_refs, out_refs2, acc_ref2, nk=nk, epilogue=epilogue)


def _matmul(name, pairs, *, n, col_off, tm, tn, nk, epilogue, rows=(), cols=(), outs=(), rider=None,
            acc_in_out=False):
    m = pairs[0][0].shape[0]
    assert m % tm == 0 and n % tn == 0 and col_off % tn == 0
    joff = col_off // tn
    nj = n // tn
    first_j = lambda i, j: jnp.where(i == 0, j, nj - 1)
    first_k = lambda i, k: jnp.where(i == 0, k, nk - 1)
    in_specs, args = [], []
    for x, w, row_off in pairs:
        kdim = x.shape[1]
        assert kdim % nk == 0
        tk = kdim // nk
        assert row_off % tk == 0
        in_specs += [pl.BlockSpec((tm, tk), lambda i, j, k: (i, k)),
                     pl.BlockSpec((tk, tn), lambda i, j, k, koff=row_off // tk: (k + koff, j + joff))]
        args += [x, w]
    for r in rows:
        in_specs.append(pl.BlockSpec((tm, tn), lambda i, j, k: (i, j)))
        args.append(r)
    for c in cols:
        in_specs.append(pl.BlockSpec((c.shape[0], tn), lambda i, j, k: (0, j)))
        args.append(c)
    m2 = 0
    if rider:
        xs2, rows2 = rider
        m2 = xs2[0].shape[0]
        for x2, (x, _, _) in zip(xs2, pairs):
            assert x2.shape[1] == x.shape[1]
            in_specs.append(pl.BlockSpec((m2, x.shape[1] // nk), lambda i, j, k: (0, first_k(i, k))))
            args.append(x2)
        for r in rows2:
            in_specs.append(pl.BlockSpec((m2, tn), lambda i, j, k: (0, first_j(i, j))))
            args.append(r)
    out_specs, out_shape = [], []

    def add_outs(mrows, trows, imap, jmap):
        for o in outs:
            if isinstance(o, tuple):
                out_specs.append(pl.BlockSpec((trows, tn // HEAD_DIM, HEAD_DIM),
                                              lambda i, j, k: (imap(i), jmap(i, j), 0)))
                out_shape.append(jax.ShapeDtypeStruct((mrows, n // HEAD_DIM, HEAD_DIM), o[0]))
            else:
                out_specs.append(pl.BlockSpec((trows, tn), lambda i, j, k: (imap(i), jmap(i, j))))
                out_shape.append(jax.ShapeDtypeStruct((mrows, n), o))

    add_outs(m, tm, lambda i: i, lambda i, j: j)
    if rider:
        add_outs(m2, m2, lambda i: 0, first_j)
    scratch = []
    if nk > 1:
        assert not acc_in_out or (len(outs) == 1 and outs[0] == F32)
        scratch = ([] if acc_in_out else [pltpu.VMEM((tm, tn), F32)]) + ([pltpu.VMEM((m2, tn), F32)] if rider else [])
    body = functools.partial(_matmul_body, n_pairs=len(pairs), n_rows=len(rows), n_cols=len(cols),
                             n_outs=len(outs), nk=nk, epilogue=epilogue, rider=bool(rider),
                             acc_in_out=acc_in_out and nk > 1)
    return pl.pallas_call(
        body,
        grid=(m // tm, nj, nk),
        in_specs=in_specs,
        out_specs=out_specs,
        out_shape=out_shape,
        scratch_shapes=scratch,
        compiler_params=_params(("arbitrary", "arbitrary", "arbitrary")),
        name=name,
    )(*args)


def _store_head(o, h, y):
    if len(o.shape) == 3:
        o[:, h, :] = y.astype(o.dtype)
    else:
        o[:, h * HEAD_DIM:(h + 1) * HEAD_DIM] = y.astype(o.dtype)


def _epi_head_norm(acc, rows, cols, outs):
    g = cols[0][...]
    for h in range(acc.shape[1] // HEAD_DIM):
        sl = slice(h * HEAD_DIM, (h + 1) * HEAD_DIM)
        blk = acc[:, sl]
        y = blk * lax.rsqrt(jnp.mean(blk * blk, axis=-1, keepdims=True) + EPS) * g[:, sl]
        for o in outs:
            _store_head(o, h, y)


def _epi_copy(acc, rows, cols, outs):
    for h in range(acc.shape[1] // HEAD_DIM):
        for o in outs:
            _store_head(o, h, acc[:, h * HEAD_DIM:(h + 1) * HEAD_DIM])


def _epi_silu(acc, rows, cols, outs):
    outs[0][...] = (acc * jax.nn.sigmoid(acc)).astype(outs[0].dtype)


def _epi_sigmoid(acc, rows, cols, outs):
    outs[0][...] = jax.nn.sigmoid(acc).astype(outs[0].dtype)


def _epi_forget(acc, rows, cols, outs, *, layer):
    logits = cols[0][...]
    e = jnp.exp(logits - jnp.max(logits, axis=0, keepdims=True))
    lb = jnp.sum(e[:layer + 1], axis=0, keepdims=True) / jnp.sum(e, axis=0, keepdims=True)
    forget = lb + (1.0 - lb) * jax.nn.sigmoid(acc)
    outs[0][...] = jnp.log(forget)
    outs[1][...] = 1.0 - forget


def _epi_residual(acc, rows, cols, outs):
    outs[0][...] = rows[0][...] + acc


def _epi_relu2(acc, rows, cols, outs):
    r = jnp.maximum(acc, 0.0)
    outs[0][...] = (r * r).astype(outs[0].dtype)


def _select_blocks_t(sc, n_valid, topk):
    nb = sc.shape[0]
    row = lax.broadcasted_iota(jnp.int32, sc.shape, 0)
    valid = row < n_valid
    sel = jnp.zeros(sc.shape, F32)
    for j in range(nb):
        sj = sc[j:j + 1, :]
        beats = ((sc > sj) | ((sc == sj) & (row < j))) & valid
        rank = jnp.sum(jnp.where(beats, 1.0, 0.0), axis=0, keepdims=True)
        chosen = jnp.where((rank < float(topk)) & (j < n_valid), 1.0, 0.0)
        sel = jnp.where(row == j, chosen, sel)
    return sel


def _moba_prompt_tile(i, q_ref, k_ref, vt_sc, o_ref, kmean, *, scale):
    blk = MOBA_BLOCK
    rows = KV_GROUP * blk
    qb = q_ref[i * blk:(i + 1) * blk, :]
    q2 = jnp.concatenate([qb[:, g * HEAD_DIM:(g + 1) * HEAD_DIM] for g in range(KV_GROUP)], axis=0)
    sel_t = _select_blocks_t(_nt_dot(kmean, q2), i, MOBA_TOPK)
    bias_t = jnp.where(sel_t > 0.5, 0.0, NEG)

    def scores(j):
        s = _nt_dot(k_ref[j * blk:(j + 1) * blk, :], q2)
        if j == i:
            kpos = lax.broadcasted_iota(jnp.int32, s.shape, 0)
            qpos = lax.broadcasted_iota(jnp.int32, s.shape, 1) % blk
            return jnp.where(kpos <= qpos, s, NEG)
        return s + bias_t[j:j + 1, :]

    blocks = list(range(i + 1))
    mrun = scores(i)
    for j in blocks[:-1]:
        mrun = jnp.maximum(mrun, scores(j))
    m = jnp.max(mrun, axis=0, keepdims=True)
    lsum = None
    acc = None
    for j in blocks:
        pe = jnp.exp2((scores(j) - m) * (scale * LOG2E))
        lsum = pe if lsum is None else lsum + pe
        d = jnp.dot(vt_sc[:, j * blk:(j + 1) * blk], pe.astype(BF16), preferred_element_type=F32)
        acc = d if acc is None else acc + d
    o = (acc / jnp.sum(lsum, axis=0, keepdims=True)).T
    for g in range(KV_GROUP):
        o_ref[i * blk:(i + 1) * blk, g * HEAD_DIM:(g + 1) * HEAD_DIM] = o[g * blk:(g + 1) * blk].astype(o_ref.dtype)


def _moba_prompt_body(q_ref, k_ref, v_ref, *rest, nb, scale, side_cast):
    w_refs = rest[:side_cast]
    o_ref = rest[side_cast]
    wb_refs = rest[side_cast + 1:2 * side_cast + 1]
    kmean_sc, vt_sc = rest[2 * side_cast + 1:]
    for w_ref, wb_ref in zip(w_refs, wb_refs):
        wb_ref[...] = w_ref[...].astype(BF16)
    p = pl.program_id(2)

    @pl.when(p == 0)
    def _():
        kall = k_ref[...].astype(F32)
        kmean_sc[...] = jnp.sum(kall.reshape(nb, MOBA_BLOCK, HEAD_DIM), axis=1) * (1.0 / MOBA_BLOCK)
        vt_sc[...] = v_ref[...].astype(F32).T.astype(BF16)

    for c in range(nb // 2):
        @pl.when(p == c)
        def _(c=c):
            kmean = kmean_sc[...].astype(BF16)
            for i in (c, nb - 1 - c):
                _moba_prompt_tile(i, q_ref, k_ref, vt_sc, o_ref, kmean, scale=scale)


def _moba_prompt(q, kb, vb, *, batch, seq, to_bf16=()):
    assert seq % (2 * MOBA_BLOCK) == 0, "query tiles are processed in (p, nb-1-p) pairs"
    nb = seq // MOBA_BLOCK
    assert nb <= LANES
    n_kv = kb.shape[1] // HEAD_DIM
    gw = KV_GROUP * HEAD_DIM
    npair = nb // 2
    steps = batch * n_kv * npair
    mats = list(to_bf16) if all(m.shape[0] % (steps * 16) == 0 for m in to_bf16) else []
    in_specs = [pl.BlockSpec((seq, gw), lambda b, n, p: (b, n)),
                pl.BlockSpec((seq, HEAD_DIM), lambda b, n, p: (b, n)),
                pl.BlockSpec((seq, HEAD_DIM), lambda b, n, p: (b, n))]
    out_specs = [pl.BlockSpec((seq, gw), lambda b, n, p: (b, n))]
    out_shape = [jax.ShapeDtypeStruct(q.shape, BF16)]
    args = [q, kb, vb]
    for m in mats:
        slab = pl.BlockSpec((m.shape[0] // steps, m.shape[1]), lambda b, n, p: ((b * n_kv + n) * npair + p, 0))
        in_specs.append(slab)
        out_specs.append(slab)
        out_shape.append(jax.ShapeDtypeStruct(m.shape, BF16))
        args.append(m)
    body = functools.partial(_moba_prompt_body, nb=nb, scale=HEAD_DIM ** -0.5, side_cast=len(mats))
    res = pl.pallas_call(
        body,
        grid=(batch, n_kv, npair),
        in_specs=in_specs,
        out_specs=out_specs,
        out_shape=out_shape,
        scratch_shapes=[pltpu.VMEM((nb, HEAD_DIM), F32),
                        pltpu.VMEM((HEAD_DIM, seq), BF16)],
        compiler_params=_params(("arbitrary", "arbitrary", "arbitrary")),
        name="moba_prompt",
    )(*args)
    return res[0], (list(res[1:]) if mats else [None] * len(to_bf16))


def _moba_sample_body(pt_ref, q_ref, kn_ref, vn_ref, *rest, pps, nch, n_kv, t_new, n_pages, scale):
    k_refs = rest[:pps]
    v_refs = rest[pps:2 * pps]
    o_ref = rest[2 * pps]
    qbd_sc, s_sc, p_sc, km_sc, oacc_sc, l_sc = rest[2 * pps + 1:]
    c = pl.program_id(1)
    ppb = MOBA_BLOCK // PAGE_SIZE
    nb = n_pages // ppb
    gt = KV_GROUP * t_new
    rows = n_kv * gt
    kvw = n_kv * HEAD_DIM

    @pl.when(c == 0)
    def _():
        qv = q_ref[...].astype(F32)
        per_g = [jnp.concatenate([qv[:, (KV_GROUP * n + g) * HEAD_DIM:(KV_GROUP * n + g + 1) * HEAD_DIM]
                                  for n in range(n_kv)], axis=1) for g in range(KV_GROUP)]
        tiled = jnp.concatenate([per_g[g] for n in range(n_kv) for g in range(KV_GROUP)], axis=0)
        rown = lax.broadcasted_iota(jnp.int32, (rows, kvw), 0) // gt
        coln = lax.broadcasted_iota(jnp.int32, (rows, kvw), 1) // HEAD_DIM
        qbd_sc[...] = jnp.where(rown == coln, tiled, 0.0).astype(BF16)

    def page_rows(ref):
        n_tok = ref.shape[0] // n_kv
        return jnp.concatenate([ref[pl.ds(n, n_tok, stride=n_kv), :] for n in range(n_kv)], axis=1)

    @pl.when(c < nch)
    def _():
        qbd = qbd_sc[...]
        colsum = None
        for i in range(pps):
            cs = jnp.sum(k_refs[i][...].reshape(PAGE_SIZE, n_kv, HEAD_DIM), axis=0)
            colsum = cs if i % ppb == 0 else colsum + cs
            if i % ppb == ppb - 1:
                km_sc[pl.ds(c * (pps // ppb) + i // ppb, 1)] = colsum[None]
            col0 = pl.multiple_of((c * pps + i) * PAGE_SIZE, PAGE_SIZE)
            s_sc[:, pl.ds(col0, PAGE_SIZE)] = _nt_dot(qbd, page_rows(k_refs[i]).astype(BF16))

    @pl.when(c == nch - 1)
    def _():
        qbd = qbd_sc[...]
        kmean = (jnp.concatenate([km_sc[:, n, :] for n in range(n_kv)], axis=1)
                 * (1.0 / MOBA_BLOCK)).astype(BF16)
        sc_t = _nt_dot(kmean, qbd)
        sel_t = _select_blocks_t(sc_t, nb, MOBA_TOPK)
        bias = jnp.where(sel_t > 0.5, 0.0, NEG).T
        zpad = jnp.zeros((LANES - t_new, kvw), F32)
        knp = jnp.concatenate([kn_ref[...].astype(F32), zpad], axis=0).astype(BF16)
        vnp = jnp.concatenate([vn_ref[...].astype(F32), zpad], axis=0).astype(BF16)
        s_own = _nt_dot(qbd, knp)
        tok = lax.broadcasted_iota(jnp.int32, s_own.shape, 0) % t_new
        key = lax.broadcasted_iota(jnp.int32, s_own.shape, 1)
        s_own = jnp.where(key <= tok, s_own, NEG)
        m = jnp.max(s_own, axis=-1, keepdims=True)
        for j in range(nb):
            sl = slice(j * MOBA_BLOCK, (j + 1) * MOBA_BLOCK)
            sb = s_sc[:, sl] + bias[:, j:j + 1]
            s_sc[:, sl] = sb
            m = jnp.maximum(m, jnp.max(sb, axis=-1, keepdims=True))
        p_own = jnp.exp((s_own - m) * scale)
        l = jnp.sum(p_own, axis=-1, keepdims=True)
        for j in range(nb):
            sl = slice(j * MOBA_BLOCK, (j + 1) * MOBA_BLOCK)
            p = jnp.exp((s_sc[:, sl] - m) * scale)
            l = l + jnp.sum(p, axis=-1, keepdims=True)
            p_sc[:, sl] = p.astype(BF16)
        l_sc[...] = jnp.broadcast_to(l, l_sc.shape)
        oacc_sc[...] = jnp.dot(p_own.astype(BF16), vnp, preferred_element_type=F32)

    @pl.when(c >= nch)
    def _():
        acc = oacc_sc[...]
        for i in range(pps):
            col0 = pl.multiple_of(((c - nch) * pps + i) * PAGE_SIZE, PAGE_SIZE)
            acc = acc + jnp.dot(p_sc[:, pl.ds(col0, PAGE_SIZE)], page_rows(v_refs[i]).astype(BF16),
                                preferred_element_type=F32)
        oacc_sc[...] = acc

    @pl.when(c == 2 * nch - 1)
    def _():
        o = oacc_sc[...] / l_sc[:, :1]
        for n in range(n_kv):
            for g in range(KV_GROUP):
                r0 = n * gt + g * t_new
                h = KV_GROUP * n + g
                o_ref[:, h * HEAD_DIM:(h + 1) * HEAD_DIM] = (
                    o[r0:r0 + t_new, n * HEAD_DIM:(n + 1) * HEAD_DIM].astype(o_ref.dtype))


def _moba_sample(q, kn, vn, cache_k, cache_v, page_table, *, batch, t_new):
    n_pages = page_table.shape[1]
    ppb = MOBA_BLOCK // PAGE_SIZE
    assert n_pages % ppb == 0 and n_pages // ppb >= MOBA_TOPK
    kvw = kn.shape[1]
    n_kv = kvw // HEAD_DIM
    pps = _tile(n_pages, 16, ppb)
    nch = n_pages // pps
    rows = n_kv * KV_GROUP * t_new
    past = n_pages * PAGE_SIZE

    def k_map(i):
        return lambda b, c, pt: (pt[b, jnp.minimum(c, nch - 1) * pps + i], 0)

    def v_map(i):
        return lambda b, c, pt: (pt[b, jnp.maximum(c - nch, 0) * pps + i], 0)

    in_specs = [pl.BlockSpec((t_new, q.shape[1]), lambda b, c, pt: (b, 0)),
                pl.BlockSpec((t_new, kvw), lambda b, c, pt: (b, 0)),
                pl.BlockSpec((t_new, kvw), lambda b, c, pt: (b, 0))]
    in_specs += [pl.BlockSpec((PAGE_SIZE * n_kv, HEAD_DIM), k_map(i)) for i in range(pps)]
    in_specs += [pl.BlockSpec((PAGE_SIZE * n_kv, HEAD_DIM), v_map(i)) for i in range(pps)]
    body = functools.partial(_moba_sample_body, pps=pps, nch=nch, n_kv=n_kv, t_new=t_new,
                             n_pages=n_pages, scale=HEAD_DIM ** -0.5)
    grid_spec = pltpu.PrefetchScalarGridSpec(
        num_scalar_prefetch=1,
        grid=(batch, 2 * nch),
        in_specs=in_specs,
        out_specs=pl.BlockSpec((t_new, q.shape[1]), lambda b, c, pt: (b, 0)),
        scratch_shapes=[pltpu.VMEM((rows, kvw), BF16),
                        pltpu.VMEM((rows, past), F32),
                        pltpu.VMEM((rows, past), BF16),
                        pltpu.VMEM((n_pages // ppb, n_kv, HEAD_DIM), F32),
                        pltpu.VMEM((rows, kvw), F32),
                        pltpu.VMEM((rows, LANES), F32)],
    )
    return pl.pallas_call(
        body,
        grid_spec=grid_spec,
        out_shape=jax.ShapeDtypeStruct(q.shape, BF16),
        compiler_params=_params(("arbitrary", "arbitrary")),
        name="moba_sample",
    )(page_table, q, kn, vn, *([cache_k] * pps), *([cache_v] * pps))


def _hgrn_body(q_ref, lf_ref, k_ref, v_ref, gate_ref, gn_ref, h0_ref, o_ref, hout_ref, ht_sc,
               *, hb, chunk, sub, n_chunks, n_tsteps):
    t = pl.program_id(2)
    ns = chunk // sub

    @pl.when(t == 0)
    def _():
        for h in range(hb):
            ht_sc[h] = h0_ref[h].T

    r_i = lax.broadcasted_iota(jnp.int32, (chunk, chunk), 0)
    c_i = lax.broadcasted_iota(jnp.int32, (chunk, chunk), 1)
    causal = c_i <= r_i
    tri = jnp.where(causal, 1.0, 0.0).astype(BF16)
    width = hb * HEAD_DIM
    rowc = lax.broadcasted_iota(jnp.int32, (chunk, width), 0)
    heads = [slice(h * HEAD_DIM, (h + 1) * HEAD_DIM) for h in range(hb)]

    def one_chunk(ci, carry):
        rs = pl.ds(pl.multiple_of(ci * chunk, chunk), chunk)
        lf = lf_ref[rs, :]
        q = q_ref[rs, :]
        k = k_ref[rs, :]
        v = v_ref[rs, :]
        hi = lf.astype(BF16)
        lo = (lf - hi.astype(F32)).astype(BF16)
        bb = jnp.dot(tri, jnp.concatenate([hi, lo], axis=1), preferred_element_type=F32)
        b = bb[:, :width] + bb[:, width:]
        b_last = b[chunk - 1:chunk, :]
        qe = (q * jnp.exp(b)).astype(BF16)
        kd = (k * jnp.exp(b_last - b)).astype(BF16)
        decay = jnp.exp(b_last)
        refs = [jnp.zeros((1, width), F32)] + [b[sub * s - 1:sub * s, :] for s in range(1, ns)]
        gfull = jnp.concatenate([jnp.broadcast_to(r, (sub, width)) for r in refs], axis=0)
        qh = q * jnp.exp(b - gfull)
        qparts = [jnp.where((rowc >= sub * s) & (rowc < sub * (s + 1)), qh, 0.0).astype(BF16) for s in range(ns)]
        kparts = [jnp.where(rowc < sub * (s + 1), k * jnp.exp(refs[s] - b), 0.0).astype(BF16) for s in range(ns)]
        hts = [ht_sc[h] for h in range(hb)]
        o_state = [_nt_dot(qe[:, cs], hts[h].astype(BF16)) for h, cs in enumerate(heads)]
        att = [_nt_dot(jnp.concatenate([p[:, cs] for p in qparts], axis=1),
                       jnp.concatenate([p[:, cs] for p in kparts], axis=1)) for cs in heads]
        upd = [_tn_dot(v[:, cs], kd[:, cs]) for cs in heads]
        o_att = [jnp.dot(jnp.where(causal, att[h], 0.0).astype(BF16), v[:, cs], preferred_element_type=F32)
                 for h, cs in enumerate(heads)]
        for h, cs in enumerate(heads):
            ht_sc[h] = hts[h] * decay[:, cs] + upd[h]
            o = o_state[h] + o_att[h]
            y = o * lax.rsqrt(jnp.mean(o * o, axis=-1, keepdims=True) + EPS) * gn_ref[:, cs]
            o_ref[rs, cs] = (y * gate_ref[rs, cs]).astype(o_ref.dtype)
        return carry

    lax.fori_loop(0, n_chunks, one_chunk, 0)

    @pl.when(t == n_tsteps - 1)
    def _():
        for h in range(hb):
            hout_ref[h] = ht_sc[h].T


def _hgrn(hq, lf, hk, hv, gate, gnorm, h0, *, batch, seq):
    width = hq.shape[1]
    n_heads = width // HEAD_DIM
    hb = _tile(n_heads, 16)
    chunk = HG_CHUNK if seq % HG_CHUNK == 0 else seq
    sub = HG_SUB if chunk % HG_SUB == 0 else chunk
    tc = _tile(seq, 256, chunk)
    nt = seq // tc
    bw = hb * HEAD_DIM
    row_spec = pl.BlockSpec((tc, bw), lambda b, g, t: (b * nt + t, g))
    st_spec = pl.BlockSpec((None, hb, HEAD_DIM, HEAD_DIM), lambda b, g, t: (b, g, 0, 0))
    body = functools.partial(_hgrn_body, hb=hb, chunk=chunk, sub=sub, n_chunks=tc // chunk, n_tsteps=nt)
    return pl.pallas_call(
        body,
        grid=(batch, n_heads // hb, nt),
        in_specs=[row_spec, row_spec, row_spec, row_spec, row_spec,
                  pl.BlockSpec((1, bw), lambda b, g, t: (0, g)), st_spec],
        out_specs=[row_spec, st_spec],
        out_shape=[jax.ShapeDtypeStruct(hq.shape, BF16), jax.ShapeDtypeStruct(h0.shape, F32)],
        scratch_shapes=[pltpu.VMEM((hb, HEAD_DIM, HEAD_DIM), F32)],
        compiler_params=_params(("arbitrary", "arbitrary", "arbitrary")),
        name="hgrn",
    )(hq, lf, hk, hv, gate, gnorm, h0)


_IN_NAMES = ("q", "k", "kb", "v", "vb", "hq", "lf", "hk", "hv", "gate")


def _layer_inputs(xp, xs, w, *, tm):
    d = xp.shape[1]
    attn_w = d // 2
    kv_w = attn_w // KV_GROUP
    hg_w = d - attn_w
    xn_p = _rmsnorm(xp, w["norm1_g"], _tile(xp.shape[0], 256, 8))
    xn_s = _rmsnorm(xs, w["norm1_g"], _tile(xs.shape[0], 256, 8))
    starts = [0, attn_w, attn_w + kv_w, attn_w + 2 * kv_w, attn_w + 2 * kv_w + hg_w,
              attn_w + 2 * kv_w + 2 * hg_w, attn_w + 2 * kv_w + 3 * hg_w]

    def proj(name, seg, n, epilogue, cols=(), outs=(), whole=False):
        tn = n if whole else _tile(n, 1024 if len(outs) == 1 else 512, LANES)
        while starts[seg] % tn:
            tn = _tile(n, tn - 1, LANES)
        assert tn == n or not whole
        tm_call = _tile(xn_p.shape[0], min(tm, KV_TM), 8) if whole else tm
        return _matmul(name, [(xn_p, w["w_in"], 0)], n=n, col_off=starts[seg], tm=tm_call, tn=tn, nk=1,
                       epilogue=epilogue, cols=cols, outs=outs, rider=([xn_s], []))

    res = []
    res += proj("proj_q", 0, attn_w, _epi_head_norm, cols=(w["q_gain"],), outs=(BF16,))
    res += proj("proj_k", 1, kv_w, _epi_head_norm, cols=(w["k_gain"],), outs=((F32, "heads"), BF16), whole=True)
    res += proj("proj_v", 2, kv_w, _epi_copy, outs=((F32, "heads"), BF16), whole=True)
    res += proj("proj_hq", 3, hg_w, _epi_silu, outs=(F32,))
    res += proj("proj_hf", 4, hg_w, functools.partial(_epi_forget, layer=0), cols=(w["lb_logits"],),
                outs=(F32, F32))
    res += proj("proj_hi", 5, hg_w, _epi_copy, outs=(BF16,))
    res += proj("proj_hg", 6, hg_w, _epi_sigmoid, outs=(F32,))
    counts = (1, 2, 2, 1, 2, 1, 1)
    prompt, sample, at = [], [], 0
    for c in counts:
        prompt += res[at:at + c]
        sample += res[at + c:at + 2 * c]
        at += 2 * c
    return dict(zip(_IN_NAMES, prompt)), dict(zip(_IN_NAMES, sample))


def _layer_outputs(xp, xs, attn_p, attn_s, hg_p, hg_s, w, *, tm):
    d = xp.shape[1]
    d_ff = w["w_up"].shape[1]
    aw = attn_p.shape[1]
    x1_p, x1_s = _matmul("proj_out", [(attn_p, w["w_out"], 0), (hg_p, w["w_out"], aw)], n=d, col_off=0, tm=tm,
                         tn=_tile(d, 512, LANES), nk=1, epilogue=_epi_residual, rows=(xp,), outs=(F32,),
                         rider=([attn_s, hg_s], [xs]))
    hn_p = _rmsnorm(x1_p, w["norm2_g"], _tile(xp.shape[0], 256, 8))
    hn_s = _rmsnorm(x1_s, w["norm2_g"], _tile(xs.shape[0], 256, 8))
    act_p, act_s = _matmul("mlp_up", [(hn_p, w["w_up"], 0)], n=d_ff, col_off=0, tm=tm, tn=_tile(d_ff, 512, LANES),
                           nk=1, epilogue=_epi_relu2, outs=(BF16,), rider=([hn_s], []))
    nk = d_ff // _tile(d_ff, MLP_DOWN_TK, LANES)
    y_p, y_s = _matmul("mlp_down", [(act_p, w["w_down"], 0)], n=d, col_off=0, tm=tm, tn=_tile(d, 1024, LANES),
                       nk=nk, epilogue=_epi_residual, rows=(x1_p,), outs=(F32,), rider=([act_s], [x1_s]),
                       acc_in_out=True)
    return y_p, y_s


def kernel(x_prompt, x_sample, cache_k, cache_v, state_h, page_table, norm1_g, w_in, q_norm_g, k_norm_g,
           lb_logits, hg_norm_g, w_out, norm2_g, w_up, w_down):
    depth = w_in.shape[0]
    assert depth == 1, "one layer"
    bp, tp, d = x_prompt.shape
    bs, ts, _ = x_sample.shape
    attn_w = d // 2
    n_q = attn_w // HEAD_DIM
    n_kv = n_q // KV_GROUP
    kv_w = n_kv * HEAD_DIM
    hg_w = d - attn_w
    n_hg = hg_w // HEAD_DIM

    w = dict(
        norm1_g=norm1_g[0], norm2_g=norm2_g[0],
        w_in=w_in[0].astype(BF16), w_out=w_out[0], w_up=w_up[0],
        q_gain=jnp.tile(q_norm_g[0], n_q).reshape(1, attn_w),
        k_gain=jnp.tile(k_norm_g[0], n_kv).reshape(1, kv_w),
        lb_logits=lb_logits,
    )
    gnorm = hg_norm_g[0].reshape(1, hg_w)

    xp = x_prompt.reshape(bp * tp, d)
    xs = x_sample.reshape(bs * ts, d)
    tm = _tile(bp * tp, 1024, 8)
    gp, gs = _layer_inputs(xp, xs, w, tm=tm)

    attn_p, (w_down_b, w_out_b) = _moba_prompt(gp["q"], gp["kb"], gp["vb"], batch=bp, seq=tp,
                                               to_bf16=(w_down[0], w_out[0]))
    w["w_down"] = w_down[0].astype(BF16) if w_down_b is None else w_down_b
    w["w_out"] = w_out[0] if w_out_b is None else w_out_b
    h0_p = jnp.zeros((bp, n_hg, HEAD_DIM, HEAD_DIM), F32)
    hg_p, h_p = _hgrn(gp["hq"], gp["lf"], gp["hk"], gp["hv"], gp["gate"], gnorm, h0_p, batch=bp, seq=tp)
    attn_s = _moba_sample(gs["q"], gs["kb"], gs["vb"], cache_k[0].reshape(-1, HEAD_DIM),
                          cache_v[0].reshape(-1, HEAD_DIM), page_table, batch=bs, t_new=ts)
    hg_s, h_s = _hgrn(gs["hq"], gs["lf"], gs["hk"], gs["hv"], gs["gate"], gnorm, state_h[0], batch=bs, seq=ts)

    y_p, y_s = _layer_outputs(xp, xs, attn_p, attn_s, hg_p, hg_s, w, tm=tm)

    return (y_p.reshape(bp, tp, d), y_s.reshape(bs, ts, d),
            gp["k"].reshape(1, bp, tp, n_kv, HEAD_DIM), gp["v"].reshape(1, bp, tp, n_kv, HEAD_DIM),
            h_p.reshape(1, bp, n_hg, HEAD_DIM, HEAD_DIM),
            gs["k"].reshape(1, bs, ts, n_kv, HEAD_DIM), gs["v"].reshape(1, bs, ts, n_kv, HEAD_DIM),
            h_s.reshape(1, bs, n_hg, HEAD_DIM, HEAD_DIM))
```

```python
import functools

import jax
import jax.numpy as jnp
from jax import lax
from jax.experimental import pallas as pl
from jax.experimental.pallas import tpu as pltpu

F32 = jnp.float32
BF16 = jnp.bfloat16

HEAD_DIM = 128
KV_GROUP = 2
PAGE_SIZE = 128
MOBA_BLOCK = 256
MOBA_TOPK = 3
HG_CHUNK = 64
HG_SUB = 16
EPS = 1e-6
NEG = -1e30
LOG2E = 1.4426950408889634
LANES = 128
SUB_N = 512
MLP_DOWN_TK = 2048
KV_TM = 1024
VMEM_LIMIT = 60 * 1024 * 1024


def _tile(n, target, mult=1):
    for t in range(min(n, target), 0, -1):
        if n % t == 0 and t % mult == 0:
            return t
    return n


def _params(sem):
    return pltpu.CompilerParams(dimension_semantics=sem, vmem_limit_bytes=VMEM_LIMIT)


def _nt_dot(a, b):
    return lax.dot_general(a, b, (((1,), (1,)), ((), ())), preferred_element_type=F32)


def _tn_dot(a, b):
    return lax.dot_general(a, b, (((0,), (0,)), ((), ())), preferred_element_type=F32)


def _rmsnorm_body(x_ref, g_ref, o_ref):
    x = x_ref[...]
    ms = jnp.mean(x * x, axis=-1, keepdims=True)
    o_ref[...] = (x * lax.rsqrt(ms + EPS) * g_ref[...]).astype(o_ref.dtype)


def _rmsnorm(x, g, tm):
    m, d = x.shape
    return pl.pallas_call(
        _rmsnorm_body,
        grid=(m // tm,),
        in_specs=[pl.BlockSpec((tm, d), lambda i: (i, 0)), pl.BlockSpec((1, d), lambda i: (0, 0))],
        out_specs=pl.BlockSpec((tm, d), lambda i: (i, 0)),
        out_shape=jax.ShapeDtypeStruct((m, d), BF16),
        compiler_params=_params(("arbitrary",)),
        name="rmsnorm",
    )(x, g.reshape(1, d))


def _col_view(ref, c0, width):
    if len(ref.shape) == 3:
        return ref.at[:, c0 // HEAD_DIM:(c0 + width) // HEAD_DIM, :]
    return ref.at[:, c0:c0 + width]


def _matmul_tile(xs, ws, row_refs, col_refs, out_refs, acc_ref, *, nk, epilogue):
    tn = ws[0].shape[1]
    if nk == 1:
        sub = SUB_N if tn % SUB_N == 0 else tn
        for c0 in range(0, tn, sub):
            part = None
            for x_ref, w_ref in zip(xs, ws):
                d = jnp.dot(x_ref[...], w_ref[:, c0:c0 + sub].astype(BF16), preferred_element_type=F32)
                part = d if part is None else part + d
            epilogue(part, [_col_view(r, c0, sub) for r in row_refs], [_col_view(r, c0, sub) for r in col_refs],
                     [_col_view(r, c0, sub) for r in out_refs])
        return
    k = pl.program_id(2)

    @pl.when(k == 0)
    def _():
        acc_ref[...] = jnp.zeros(acc_ref.shape, F32)

    part = acc_ref[...]
    for x_ref, w_ref in zip(xs, ws):
        part = part + jnp.dot(x_ref[...], w_ref[...].astype(BF16), preferred_element_type=F32)
    acc_ref[...] = part

    @pl.when(k == nk - 1)
    def _():
        epilogue(acc_ref[...], row_refs, col_refs, out_refs)


def _matmul_body(*refs, n_pairs, n_rows, n_cols, n_outs, nk, epilogue, rider, acc_in_out):
    it = iter(refs)
    take = lambda cnt: [next(it) for _ in range(cnt)]
    xw = take(2 * n_pairs)
    xs, ws = xw[0::2], xw[1::2]
    row_refs, col_refs = take(n_rows), take(n_cols)
    xs2, row_refs2 = (take(n_pairs), take(n_rows)) if rider else ([], [])
    out_refs = take(n_outs)
    out_refs2 = take(n_outs) if rider else []
    acc_ref = out_refs[0] if acc_in_out else (next(it) if nk > 1 else None)
    acc_ref2 = next(it) if rider and nk > 1 else None
    _matmul_tile(xs, ws, row_refs, col_refs, out_refs, acc_ref, nk=nk, epilogue=epilogue)
    if rider:
        @pl.when(pl.program_id(0) == 0)
        def _():
            _matmul_tile(xs2, ws, row_refs2, col_refs, out_refs2, acc_ref2, nk=nk, epilogue=epilogue)


def _matmul(name, pairs, *, n, col_off, tm, tn, nk, epilogue, rows=(), cols=(), outs=(), rider=None,
            acc_in_out=False):
    m = pairs[0][0].shape[0]
    assert m % tm == 0 and n % tn == 0 and col_off % tn == 0
    joff = col_off // tn
    nj = n // tn
    first_j = lambda i, j: jnp.where(i == 0, j, nj - 1)
    first_k = lambda i, k: jnp.where(i == 0, k, nk - 1)
    in_specs, args = [], []
    for x, w, row_off in pairs:
        kdim = x.shape[1]
        assert kdim % nk == 0
        tk = kdim // nk
        assert row_off % tk == 0
        in_specs += [pl.BlockSpec((tm, tk), lambda i, j, k: (i, k)),
                     pl.BlockSpec((tk, tn), lambda i, j, k, koff=row_off // tk: (k + koff, j + joff))]
        args += [x, w]
    for r in rows:
        in_specs.append(pl.BlockSpec((tm, tn), lambda i, j, k: (i, j)))
        args.append(r)
    for c in cols:
        in_specs.append(pl.BlockSpec((c.shape[0], tn), lambda i, j, k: (0, j)))
        args.append(c)
    m2 = 0
    if rider:
        xs2, rows2 = rider
        m2 = xs2[0].shape[0]
        for x2, (x, _, _) in zip(xs2, pairs):
            assert x2.shape[1] == x.shape[1]
            in_specs.append(pl.BlockSpec((m2, x.shape[1] // nk), lambda i, j, k: (0, first_k(i, k))))
            args.append(x2)
        for r in rows2:
            in_specs.append(pl.BlockSpec((m2, tn), lambda i, j, k: (0, first_j(i, j))))
            args.append(r)
    out_specs, out_shape = [], []

    def add_outs(mrows, trows, imap, jmap):
        for o in outs:
            if isinstance(o, tuple):
                out_specs.append(pl.BlockSpec((trows, tn // HEAD_DIM, HEAD_DIM),
                                              lambda i, j, k: (imap(i), jmap(i, j), 0)))
                out_shape.append(jax.ShapeDtypeStruct((mrows, n // HEAD_DIM, HEAD_DIM), o[0]))
            else:
                out_specs.append(pl.BlockSpec((trows, tn), lambda i, j, k: (imap(i), jmap(i, j))))
                out_shape.append(jax.ShapeDtypeStruct((mrows, n), o))

    add_outs(m, tm, lambda i: i, lambda i, j: j)
    if rider:
        add_outs(m2, m2, lambda i: 0, first_j)
    scratch = []
    if nk > 1:
        assert not acc_in_out or (len(outs) == 1 and outs[0] == F32)
        scratch = ([] if acc_in_out else [pltpu.VMEM((tm, tn), F32)]) + ([pltpu.VMEM((m2, tn), F32)] if rider else [])
    body = functools.partial(_matmul_body, n_pairs=len(pairs), n_rows=len(rows), n_cols=len(cols),
                             n_outs=len(outs), nk=nk, epilogue=epilogue, rider=bool(rider),
                             acc_in_out=acc_in_out and nk > 1)
    return pl.pallas_call(
        body,
        grid=(m // tm, nj, nk),
        in_specs=in_specs,
        out_specs=out_specs,
        out_shape=out_shape,
        scratch_shapes=scratch,
        compiler_params=_params(("arbitrary", "arbitrary", "arbitrary")),
        name=name,
    )(*args)


def _store_head(o, h, y):
    if len(o.shape) == 3:
        o[:, h, :] = y.astype(o.dtype)
    else:
        o[:, h * HEAD_DIM:(h + 1) * HEAD_DIM] = y.astype(o.dtype)


def _epi_head_norm(acc, rows, cols, outs):
    g = cols[0][...]
    for h in range(acc.shape[1] // HEAD_DIM):
        sl = slice(h * HEAD_DIM, (h + 1) * HEAD_DIM)
        blk = acc[:, sl]
        y = blk * lax.rsqrt(jnp.mean(blk * blk, axis=-1, keepdims=True) + EPS) * g[:, sl]
        for o in outs:
            _store_head(o, h, y)


def _epi_copy(acc, rows, cols, outs):
    for h in range(acc.shape[1] // HEAD_DIM):
        for o in outs:
            _store_head(o, h, acc[:, h * HEAD_DIM:(h + 1) * HEAD_DIM])


def _epi_silu(acc, rows, cols, outs):
    outs[0][...] = (acc * jax.nn.sigmoid(acc)).astype(outs[0].dtype)


def _epi_sigmoid(acc, rows, cols, outs):
    outs[0][...] = jax.nn.sigmoid(acc).astype(outs[0].dtype)


def _epi_forget(acc, rows, cols, outs, *, layer):
    logits = cols[0][...]
    e = jnp.exp(logits - jnp.max(logits, axis=0, keepdims=True))
    lb = jnp.sum(e[:layer + 1], axis=0, keepdims=True) / jnp.sum(e, axis=0, keepdims=True)
    forget = lb + (1.0 - lb) * jax.nn.sigmoid(acc)
    outs[0][...] = jnp.log(forget)


def _epi_residual(acc, rows, cols, outs):
    outs[0][...] = rows[0][...] + acc


def _epi_relu2(acc, rows, cols, outs):
    r = jnp.maximum(acc, 0.0)
    outs[0][...] = (r * r).astype(outs[0].dtype)


def _select_blocks_t(sc, n_valid, topk):
    nb = sc.shape[0]
    row = lax.broadcasted_iota(jnp.int32, sc.shape, 0)
    valid = row < n_valid
    sel = jnp.zeros(sc.shape, F32)
    for j in range(nb):
        sj = sc[j:j + 1, :]
        beats = ((sc > sj) | ((sc == sj) & (row < j))) & valid
        rank = jnp.sum(jnp.where(beats, 1.0, 0.0), axis=0, keepdims=True)
        chosen = jnp.where((rank < float(topk)) & (j < n_valid), 1.0, 0.0)
        sel = jnp.where(row == j, chosen, sel)
    return sel


def _moba_prompt_tile(i, q_ref, k_ref, vt_sc, o_ref, kmean, *, scale):
    blk = MOBA_BLOCK
    rows = KV_GROUP * blk
    qb = q_ref[i * blk:(i + 1) * blk, :]
    q2 = jnp.concatenate([qb[:, g * HEAD_DIM:(g + 1) * HEAD_DIM] for g in range(KV_GROUP)], axis=0)
    sel_t = _select_blocks_t(_nt_dot(kmean, q2), i, MOBA_TOPK)
    bias_t = jnp.where(sel_t > 0.5, 0.0, NEG)

    def scores(j):
        s = _nt_dot(k_ref[j * blk:(j + 1) * blk, :], q2)
        if j == i:
            kpos = lax.broadcasted_iota(jnp.int32, s.shape, 0)
            qpos = lax.broadcasted_iota(jnp.int32, s.shape, 1) % blk
            return jnp.where(kpos <= qpos, s, NEG)
        return s + bias_t[j:j + 1, :]

    blocks = list(range(i + 1))
    mrun = scores(i)
    for j in blocks[:-1]:
        mrun = jnp.maximum(mrun, scores(j))
    m = jnp.max(mrun, axis=0, keepdims=True)
    lsum = None
    acc = None
    for j in blocks:
        pe = jnp.exp2((scores(j) - m) * (scale * LOG2E))
        lsum = pe if lsum is None else lsum + pe
        d = jnp.dot(vt_sc[:, j * blk:(j + 1) * blk], pe.astype(BF16), preferred_element_type=F32)
        acc = d if acc is None else acc + d
    o = (acc / jnp.sum(lsum, axis=0, keepdims=True)).T
    for g in range(KV_GROUP):
        o_ref[i * blk:(i + 1) * blk, g * HEAD_DIM:(g + 1) * HEAD_DIM] = o[g * blk:(g + 1) * blk].astype(o_ref.dtype)


def _moba_prompt_body(q_ref, k_ref, v_ref, *rest, nb, scale, side_cast):
    if side_cast:
        w_ref, o_ref, wb_ref, kmean_sc, vt_sc = rest
        wb_ref[...] = w_ref[...].astype(BF16)
    else:
        o_ref, kmean_sc, vt_sc = rest
    p = pl.program_id(2)

    @pl.when(p == 0)
    def _():
        kall = k_ref[...].astype(F32)
        kmean_sc[...] = jnp.sum(kall.reshape(nb, MOBA_BLOCK, HEAD_DIM), axis=1) * (1.0 / MOBA_BLOCK)
        vt_sc[...] = v_ref[...].astype(F32).T.astype(BF16)

    for c in range(nb // 2):
        @pl.when(p == c)
        def _(c=c):
            kmean = kmean_sc[...].astype(BF16)
            for i in (c, nb - 1 - c):
                _moba_prompt_tile(i, q_ref, k_ref, vt_sc, o_ref, kmean, scale=scale)


def _moba_prompt(q, kb, vb, *, batch, seq, to_bf16=None):
    assert seq % (2 * MOBA_BLOCK) == 0, "query tiles are processed in (p, nb-1-p) pairs"
    nb = seq // MOBA_BLOCK
    assert nb <= LANES
    n_kv = kb.shape[1] // HEAD_DIM
    gw = KV_GROUP * HEAD_DIM
    npair = nb // 2
    steps = batch * n_kv * npair
    side_cast = to_bf16 is not None and to_bf16.shape[0] % (steps * 16) == 0
    in_specs = [pl.BlockSpec((seq, gw), lambda b, n, p: (b, n)),
                pl.BlockSpec((seq, HEAD_DIM), lambda b, n, p: (b, n)),
                pl.BlockSpec((seq, HEAD_DIM), lambda b, n, p: (b, n))]
    out_specs = [pl.BlockSpec((seq, gw), lambda b, n, p: (b, n))]
    out_shape = [jax.ShapeDtypeStruct(q.shape, BF16)]
    args = [q, kb, vb]
    if side_cast:
        slab = pl.BlockSpec((to_bf16.shape[0] // steps, to_bf16.shape[1]),
                            lambda b, n, p: ((b * n_kv + n) * npair + p, 0))
        in_specs.append(slab)
        out_specs.append(slab)
        out_shape.append(jax.ShapeDtypeStruct(to_bf16.shape, BF16))
        args.append(to_bf16)
    body = functools.partial(_moba_prompt_body, nb=nb, scale=HEAD_DIM ** -0.5, side_cast=side_cast)
    res = pl.pallas_call(
        body,
        grid=(batch, n_kv, npair),
        in_specs=in_specs,
        out_specs=out_specs,
        out_shape=out_shape,
        scratch_shapes=[pltpu.VMEM((nb, HEAD_DIM), F32),
                        pltpu.VMEM((HEAD_DIM, seq), BF16)],
        compiler_params=_params(("arbitrary", "arbitrary", "arbitrary")),
        name="moba_prompt",
    )(*args)
    return res[0], (res[1] if side_cast else None)


def _moba_sample_body(pt_ref, q_ref, kn_ref, vn_ref, *rest, pps, nch, n_kv, t_new, n_pages, scale):
    k_refs = rest[:pps]
    v_refs = rest[pps:2 * pps]
    o_ref = rest[2 * pps]
    qbd_sc, s_sc, p_sc, km_sc, oacc_sc, l_sc = rest[2 * pps + 1:]
    c = pl.program_id(1)
    ppb = MOBA_BLOCK // PAGE_SIZE
    nb = n_pages // ppb
    gt = KV_GROUP * t_new
    rows = n_kv * gt
    kvw = n_kv * HEAD_DIM

    @pl.when(c == 0)
    def _():
        qv = q_ref[...].astype(F32)
        per_g = [jnp.concatenate([qv[:, (KV_GROUP * n + g) * HEAD_DIM:(KV_GROUP * n + g + 1) * HEAD_DIM]
                                  for n in range(n_kv)], axis=1) for g in range(KV_GROUP)]
        tiled = jnp.concatenate([per_g[g] for n in range(n_kv) for g in range(KV_GROUP)], axis=0)
        rown = lax.broadcasted_iota(jnp.int32, (rows, kvw), 0) // gt
        coln = lax.broadcasted_iota(jnp.int32, (rows, kvw), 1) // HEAD_DIM
        qbd_sc[...] = jnp.where(rown == coln, tiled, 0.0).astype(BF16)

    def page_rows(ref):
        n_tok = ref.shape[0] // n_kv
        return jnp.concatenate([ref[pl.ds(n, n_tok, stride=n_kv), :] for n in range(n_kv)], axis=1)

    @pl.when(c < nch)
    def _():
        qbd = qbd_sc[...]
        colsum = None
        for i in range(pps):
            cs = jnp.sum(k_refs[i][...].reshape(PAGE_SIZE, n_kv, HEAD_DIM), axis=0)
            colsum = cs if i % ppb == 0 else colsum + cs
            if i % ppb == ppb - 1:
                km_sc[pl.ds(c * (pps // ppb) + i // ppb, 1)] = colsum[None]
            col0 = pl.multiple_of((c * pps + i) * PAGE_SIZE, PAGE_SIZE)
            s_sc[:, pl.ds(col0, PAGE_SIZE)] = _nt_dot(qbd, page_rows(k_refs[i]).astype(BF16))

    @pl.when(c == nch - 1)
    def _():
        qbd = qbd_sc[...]
        kmean = (jnp.concatenate([km_sc[:, n, :] for n in range(n_kv)], axis=1)
                 * (1.0 / MOBA_BLOCK)).astype(BF16)
        sc_t = _nt_dot(kmean, qbd)
        sel_t = _select_blocks_t(sc_t, nb, MOBA_TOPK)
        bias = jnp.where(sel_t > 0.5, 0.0, NEG).T
        zpad = jnp.zeros((LANES - t_new, kvw), F32)
        knp = jnp.concatenate([kn_ref[...].astype(F32), zpad], axis=0).astype(BF16)
        vnp = jnp.concatenate([vn_ref[...].astype(F32), zpad], axis=0).astype(BF16)
        s_own = _nt_dot(qbd, knp)
        tok = lax.broadcasted_iota(jnp.int32, s_own.shape, 0) % t_new
        key = lax.broadcasted_iota(jnp.int32, s_own.shape, 1)
        s_own = jnp.where(key <= tok, s_own, NEG)
        m = jnp.max(s_own, axis=-1, keepdims=True)
        for j in range(nb):
            sl = slice(j * MOBA_BLOCK, (j + 1) * MOBA_BLOCK)
            sb = s_sc[:, sl] + bias[:, j:j + 1]
            s_sc[:, sl] = sb
            m = jnp.maximum(m, jnp.max(sb, axis=-1, keepdims=True))
        p_own = jnp.exp((s_own - m) * scale)
        l = jnp.sum(p_own, axis=-1, keepdims=True)
        for j in range(nb):
            sl = slice(j * MOBA_BLOCK, (j + 1) * MOBA_BLOCK)
            p = jnp.exp((s_sc[:, sl] - m) * scale)
            l = l + jnp.sum(p, axis=-1, keepdims=True)
            p_sc[:, sl] = p.astype(BF16)
        l_sc[...] = jnp.broadcast_to(l, l_sc.shape)
        oacc_sc[...] = jnp.dot(p_own.astype(BF16), vnp, preferred_element_type=F32)

    @pl.when(c >= nch)
    def _():
        acc = oacc_sc[...]
        for i in range(pps):
            col0 = pl.multiple_of(((c - nch) * pps + i) * PAGE_SIZE, PAGE_SIZE)
            acc = acc + jnp.dot(p_sc[:, pl.ds(col0, PAGE_SIZE)], page_rows(v_refs[i]).astype(BF16),
                                preferred_element_type=F32)
        oacc_sc[...] = acc

    @pl.when(c == 2 * nch - 1)
    def _():
        o = oacc_sc[...] / l_sc[:, :1]
        for n in range(n_kv):
            for g in range(KV_GROUP):
                r0 = n * gt + g * t_new
                h = KV_GROUP * n + g
                o_ref[:, h * HEAD_DIM:(h + 1) * HEAD_DIM] = (
                    o[r0:r0 + t_new, n * HEAD_DIM:(n + 1) * HEAD_DIM].astype(o_ref.dtype))


def _moba_sample(q, kn, vn, cache_k, cache_v, page_table, *, batch, t_new):
    n_pages = page_table.shape[1]
    ppb = MOBA_BLOCK // PAGE_SIZE
    assert n_pages % ppb == 0 and n_pages // ppb >= MOBA_TOPK
    kvw = kn.shape[1]
    n_kv = kvw // HEAD_DIM
    pps = _tile(n_pages, 16, ppb)
    nch = n_pages // pps
    rows = n_kv * KV_GROUP * t_new
    past = n_pages * PAGE_SIZE

    def k_map(i):
        return lambda b, c, pt: (pt[b, jnp.minimum(c, nch - 1) * pps + i], 0)

    def v_map(i):
        return lambda b, c, pt: (pt[b, jnp.maximum(c - nch, 0) * pps + i], 0)

    in_specs = [pl.BlockSpec((t_new, q.shape[1]), lambda b, c, pt: (b, 0)),
                pl.BlockSpec((t_new, kvw), lambda b, c, pt: (b, 0)),
                pl.BlockSpec((t_new, kvw), lambda b, c, pt: (b, 0))]
    in_specs += [pl.BlockSpec((PAGE_SIZE * n_kv, HEAD_DIM), k_map(i)) for i in range(pps)]
    in_specs += [pl.BlockSpec((PAGE_SIZE * n_kv, HEAD_DIM), v_map(i)) for i in range(pps)]
    body = functools.partial(_moba_sample_body, pps=pps, nch=nch, n_kv=n_kv, t_new=t_new,
                             n_pages=n_pages, scale=HEAD_DIM ** -0.5)
    grid_spec = pltpu.PrefetchScalarGridSpec(
        num_scalar_prefetch=1,
        grid=(batch, 2 * nch),
        in_specs=in_specs,
        out_specs=pl.BlockSpec((t_new, q.shape[1]), lambda b, c, pt: (b, 0)),
        scratch_shapes=[pltpu.VMEM((rows, kvw), BF16),
                        pltpu.VMEM((rows, past), F32),
                        pltpu.VMEM((rows, past), BF16),
                        pltpu.VMEM((n_pages // ppb, n_kv, HEAD_DIM), F32),
                        pltpu.VMEM((rows, kvw), F32),
                        pltpu.VMEM((rows, LANES), F32)],
    )
    return pl.pallas_call(
        body,
        grid_spec=grid_spec,
        out_shape=jax.ShapeDtypeStruct(q.shape, BF16),
        compiler_params=_params(("arbitrary", "arbitrary")),
        name="moba_sample",
    )(page_table, q, kn, vn, *([cache_k] * pps), *([cache_v] * pps))


def _hgrn_body(q_ref, lf_ref, v_ref, gate_ref, gn_ref, h0_ref, o_ref, hout_ref, ht_sc,
               *, hb, chunk, sub, n_chunks, n_tsteps):
    t = pl.program_id(2)
    ns = chunk // sub

    @pl.when(t == 0)
    def _():
        for h in range(hb):
            ht_sc[h] = h0_ref[h].T

    r_i = lax.broadcasted_iota(jnp.int32, (chunk, chunk), 0)
    c_i = lax.broadcasted_iota(jnp.int32, (chunk, chunk), 1)
    causal = c_i <= r_i
    tri = jnp.where(causal, 1.0, 0.0).astype(BF16)
    width = hb * HEAD_DIM
    rowc = lax.broadcasted_iota(jnp.int32, (chunk, width), 0)
    heads = [slice(h * HEAD_DIM, (h + 1) * HEAD_DIM) for h in range(hb)]

    def one_chunk(ci, carry):
        rs = pl.ds(pl.multiple_of(ci * chunk, chunk), chunk)
        lf = lf_ref[rs, :]
        q = q_ref[rs, :]
        k = 1.0 - jnp.exp(lf)
        v = v_ref[rs, :]
        hi = lf.astype(BF16)
        lo = (lf - hi.astype(F32)).astype(BF16)
        bb = jnp.dot(tri, jnp.concatenate([hi, lo], axis=1), preferred_element_type=F32)
        b = bb[:, :width] + bb[:, width:]
        b_last = b[chunk - 1:chunk, :]
        qe = (q * jnp.exp(b)).astype(BF16)
        kd = (k * jnp.exp(b_last - b)).astype(BF16)
        decay = jnp.exp(b_last)
        refs = [jnp.zeros((1, width), F32)] + [b[sub * s - 1:sub * s, :] for s in range(1, ns)]
        gfull = jnp.concatenate([jnp.broadcast_to(r, (sub, width)) for r in refs], axis=0)
        qh = q * jnp.exp(b - gfull)
        qparts = [jnp.where((rowc >= sub * s) & (rowc < sub * (s + 1)), qh, 0.0).astype(BF16) for s in range(ns)]
        kparts = [jnp.where(rowc < sub * (s + 1), k * jnp.exp(refs[s] - b), 0.0).astype(BF16) for s in range(ns)]
        hts = [ht_sc[h] for h in range(hb)]
        o_state = [_nt_dot(qe[:, cs], hts[h].astype(BF16)) for h, cs in enumerate(heads)]
        att = [_nt_dot(jnp.concatenate([p[:, cs] for p in qparts], axis=1),
                       jnp.concatenate([p[:, cs] for p in kparts], axis=1)) for cs in heads]
        upd = [_tn_dot(v[:, cs], kd[:, cs]) for cs in heads]
        o_att = [jnp.dot(jnp.where(causal, att[h], 0.0).astype(BF16), v[:, cs], preferred_element_type=F32)
                 for h, cs in enumerate(heads)]
        for h, cs in enumerate(heads):
            ht_sc[h] = hts[h] * decay[:, cs] + upd[h]
            o = o_state[h] + o_att[h]
            y = o * lax.rsqrt(jnp.mean(o * o, axis=-1, keepdims=True) + EPS) * gn_ref[:, cs]
            o_ref[rs, cs] = (y * gate_ref[rs, cs]).astype(o_ref.dtype)
        return carry

    lax.fori_loop(0, n_chunks, one_chunk, 0)

    @pl.when(t == n_tsteps - 1)
    def _():
        for h in range(hb):
            hout_ref[h] = ht_sc[h].T


def _hgrn(hq, lf, hv, gate, gnorm, h0, *, batch, seq):
    width = hq.shape[1]
    n_heads = width // HEAD_DIM
    hb = _tile(n_heads, 16)
    chunk = HG_CHUNK if seq % HG_CHUNK == 0 else seq
    sub = HG_SUB if chunk % HG_SUB == 0 else chunk
    tc = _tile(seq, 256, chunk)
    nt = seq // tc
    bw = hb * HEAD_DIM
    row_spec = pl.BlockSpec((tc, bw), lambda b, g, t: (b * nt + t, g))
    st_spec = pl.BlockSpec((None, hb, HEAD_DIM, HEAD_DIM), lambda b, g, t: (b, g, 0, 0))
    body = functools.partial(_hgrn_body, hb=hb, chunk=chunk, sub=sub, n_chunks=tc // chunk, n_tsteps=nt)
    return pl.pallas_call(
        body,
        grid=(batch, n_heads // hb, nt),
        in_specs=[row_spec, row_spec, row_spec, row_spec,
                  pl.BlockSpec((1, bw), lambda b, g, t: (0, g)), st_spec],
        out_specs=[row_spec, st_spec],
        out_shape=[jax.ShapeDtypeStruct(hq.shape, BF16), jax.ShapeDtypeStruct(h0.shape, F32)],
        scratch_shapes=[pltpu.VMEM((hb, HEAD_DIM, HEAD_DIM), F32)],
        compiler_params=_params(("arbitrary", "arbitrary", "arbitrary")),
        name="hgrn",
    )(hq, lf, hv, gate, gnorm, h0)


_IN_NAMES = ("q", "k", "kb", "v", "vb", "hq", "lf", "hv", "gate")


def _layer_inputs(xp, xs, w, *, tm):
    d = xp.shape[1]
    attn_w = d // 2
    kv_w = attn_w // KV_GROUP
    hg_w = d - attn_w
    xn_p = _rmsnorm(xp, w["norm1_g"], _tile(xp.shape[0], 256, 8))
    xn_s = _rmsnorm(xs, w["norm1_g"], _tile(xs.shape[0], 256, 8))
    starts = [0, attn_w, attn_w + kv_w, attn_w + 2 * kv_w, attn_w + 2 * kv_w + hg_w,
              attn_w + 2 * kv_w + 2 * hg_w, attn_w + 2 * kv_w + 3 * hg_w]

    def proj(name, seg, n, epilogue, cols=(), outs=(), whole=False):
        tn = n if whole else _tile(n, 1024 if len(outs) == 1 else 512, LANES)
        while starts[seg] % tn:
            tn = _tile(n, tn - 1, LANES)
        assert tn == n or not whole
        tm_call = _tile(xn_p.shape[0], min(tm, KV_TM), 8) if whole else tm
        return _matmul(name, [(xn_p, w["w_in"], 0)], n=n, col_off=starts[seg], tm=tm_call, tn=tn, nk=1,
                       epilogue=epilogue, cols=cols, outs=outs, rider=([xn_s], []))

    res = []
    res += proj("proj_q", 0, attn_w, _epi_head_norm, cols=(w["q_gain"],), outs=(BF16,))
    res += proj("proj_k", 1, kv_w, _epi_head_norm, cols=(w["k_gain"],), outs=((F32, "heads"), BF16), whole=True)
    res += proj("proj_v", 2, kv_w, _epi_copy, outs=((F32, "heads"), BF16), whole=True)
    res += proj("proj_hq", 3, hg_w, _epi_silu, outs=(F32,))
    res += proj("proj_hf", 4, hg_w, functools.partial(_epi_forget, layer=0), cols=(w["lb_logits"],),
                outs=(F32,))
    res += proj("proj_hi", 5, hg_w, _epi_copy, outs=(BF16,))
    res += proj("proj_hg", 6, hg_w, _epi_sigmoid, outs=(F32,))
    counts = (1, 2, 2, 1, 1, 1, 1)
    prompt, sample, at = [], [], 0
    for c in counts:
        prompt += res[at:at + c]
        sample += res[at + c:at + 2 * c]
        at += 2 * c
    return dict(zip(_IN_NAMES, prompt)), dict(zip(_IN_NAMES, sample))


def _layer_outputs(xp, xs, attn_p, attn_s, hg_p, hg_s, w, *, tm):
    d = xp.shape[1]
    d_ff = w["w_up"].shape[1]
    aw = attn_p.shape[1]
    x1_p, x1_s = _matmul("proj_out", [(attn_p, w["w_out"], 0), (hg_p, w["w_out"], aw)], n=d, col_off=0, tm=tm,
                         tn=_tile(d, 512, LANES), nk=1, epilogue=_epi_residual, rows=(xp,), outs=(F32,),
                         rider=([attn_s, hg_s], [xs]))
    hn_p = _rmsnorm(x1_p, w["norm2_g"], _tile(xp.shape[0], 256, 8))
    hn_s = _rmsnorm(x1_s, w["norm2_g"], _tile(xs.shape[0], 256, 8))
    act_p, act_s = _matmul("mlp_up", [(hn_p, w["w_up"], 0)], n=d_ff, col_off=0, tm=tm, tn=_tile(d_ff, 512, LANES),
                           nk=1, epilogue=_epi_relu2, outs=(BF16,), rider=([hn_s], []))
    nk = d_ff // _tile(d_ff, MLP_DOWN_TK, LANES)
    y_p, y_s = _matmul("mlp_down", [(act_p, w["w_down"], 0)], n=d, col_off=0, tm=tm, tn=_tile(d, 1024, LANES),
                       nk=nk, epilogue=_epi_residual, rows=(x1_p,), outs=(F32,), rider=([act_s], [x1_s]),
                       acc_in_out=True)
    return y_p, y_s


def kernel(x_prompt, x_sample, cache_k, cache_v, state_h, page_table, norm1_g, w_in, q_norm_g, k_norm_g,
           lb_logits, hg_norm_g, w_out, norm2_g, w_up, w_down):
    depth = w_in.shape[0]
    assert depth == 1, "one layer"
    bp, tp, d = x_prompt.shape
    bs, ts, _ = x_sample.shape
    attn_w = d // 2
    n_q = attn_w // HEAD_DIM
    n_kv = n_q // KV_GROUP
    kv_w = n_kv * HEAD_DIM
    hg_w = d - attn_w
    n_hg = hg_w // HEAD_DIM

    w = dict(
        norm1_g=norm1_g[0], norm2_g=norm2_g[0],
        w_in=w_in[0].astype(BF16), w_out=w_out[0], w_up=w_up[0],
        q_gain=jnp.tile(q_norm_g[0], n_q).reshape(1, attn_w),
        k_gain=jnp.tile(k_norm_g[0], n_kv).reshape(1, kv_w),
        lb_logits=lb_logits,
    )
    gnorm = hg_norm_g[0].reshape(1, hg_w)

    xp = x_prompt.reshape(bp * tp, d)
    xs = x_sample.reshape(bs * ts, d)
    tm = _tile(bp * tp, 1024, 8)
    gp, gs = _layer_inputs(xp, xs, w, tm=tm)

    attn_p, w_down_b = _moba_prompt(gp["q"], gp["kb"], gp["vb"], batch=bp, seq=tp, to_bf16=w_down[0])
    w["w_down"] = w_down[0].astype(BF16) if w_down_b is None else w_down_b
    h0_p = jnp.zeros((bp, n_hg, HEAD_DIM, HEAD_DIM), F32)
    hg_p, h_p = _hgrn(gp["hq"], gp["lf"], gp["hv"], gp["gate"], gnorm, h0_p, batch=bp, seq=tp)
    attn_s = _moba_sample(gs["q"], gs["kb"], gs["vb"], cache_k[0].reshape(-1, HEAD_DIM),
                          cache_v[0].reshape(-1, HEAD_DIM), page_table, batch=bs, t_new=ts)
    hg_s, h_s = _hgrn(gs["hq"], gs["lf"], gs["hv"], gs["gate"], gnorm, state_h[0], batch=bs, seq=ts)

    y_p, y_s = _layer_outputs(xp, xs, attn_p, attn_s, hg_p, hg_s, w, tm=tm)

    return (y_p.reshape(bp, tp, d), y_s.reshape(bs, ts, d),
            gp["k"].reshape(1, bp, tp, n_kv, HEAD_DIM), gp["v"].reshape(1, bp, tp, n_kv, HEAD_DIM),
            h_p.reshape(1, bp, n_hg, HEAD_DIM, HEAD_DIM),
            gs["k"].reshape(1, bs, ts, n_kv, HEAD_DIM), gs["v"].reshape(1, bs, ts, n_kv, HEAD_DIM),
            h_s.reshape(1, bs, n_hg, HEAD_DIM, HEAD_DIM))
```
